```python
import math
import jax, jax.numpy as jnp
from jax import lax
import numpy as np

D_MODEL = 1024
BATCH = 8
SEQ = 4096
DEPTH = 2
DEC_BATCH = 8
DEC_SEQ = 2048
PAST_LEN = 128

N_MEM = 256
D_FF = 4 * D_MODEL
CROSS_HEADS = 4
CROSS_HEAD_DIM = D_MODEL // CROSS_HEADS
S5_WIDTH = D_MODEL // 2
S5_GROUP = 16
S5_GROUPS = S5_WIDTH // S5_GROUP
S5_STATE = 64
SGU_WIDTH = D_MODEL // 2
SGU_HEADS = 4
SGU_HEAD_DIM = SGU_WIDTH // SGU_HEADS
SGU_CHUNK = 128
AB_IN = S5_WIDTH + 2 * SGU_WIDTH
AB_OUT = S5_WIDTH + SGU_WIDTH
POOL_WINDOWS = (2, 4, 8, 16)
POOL_GROUPS = len(POOL_WINDOWS)
POOL_GROUP_DIM = D_MODEL // POOL_GROUPS
N_EVEN = (DEPTH + 1) // 2
N_ODD = DEPTH // 2
EPS = 1e-6

kernel_name = "hybrid_s5_sgu_pool_macaron_encoder"


def _rmsnorm(x, g):
    xf = x.astype(jnp.float32)
    y = xf * lax.rsqrt(jnp.mean(xf * xf, axis=-1, keepdims=True) + EPS)
    return (y * g.astype(jnp.float32)).astype(x.dtype)


def _layernorm(x, g, b):
    xf = x.astype(jnp.float32)
    mu = jnp.mean(xf, axis=-1, keepdims=True)
    xc = xf - mu
    var = jnp.mean(xc * xc, axis=-1, keepdims=True)
    y = xc * lax.rsqrt(var + EPS) * g.astype(jnp.float32) + b.astype(jnp.float32)
    return y.astype(x.dtype)


def _swiglu(h, w_gate, w_up, w_down):
    return (jax.nn.silu(h @ w_gate) * (h @ w_up)) @ w_down


def _complex_linear_combine(left, right):
    a1r, a1i, b1r, b1i = left
    a2r, a2i, b2r, b2i = right
    ar = a2r * a1r - a2i * a1i
    ai = a2r * a1i + a2i * a1r
    br = a2r * b1r - a2i * b1i + b2r
    bi = a2r * b1i + a2i * b1r + b2i
    return (ar, ai, br, bi)


def _s5_direction(uf, lam_re, lam_im, log_dt, b_re, b_im, c_re, c_im, reverse):
    L = uf.shape[1]
    dt = jnp.exp(log_dt)[:, None]
    mag = jnp.exp(lam_re * dt)
    ang = lam_im * dt
    ab_re = mag * jnp.cos(ang)
    ab_im = mag * jnp.sin(ang)
    nr = ab_re - 1.0
    ni = ab_im
    den = lam_re * lam_re + lam_im * lam_im
    q_re = (nr * lam_re + ni * lam_im) / den
    q_im = (ni * lam_re - nr * lam_im) / den
    bb_re = q_re[..., None] * b_re - q_im[..., None] * b_im
    bb_im = q_re[..., None] * b_im + q_im[..., None] * b_re
    x_re = jnp.einsum('blgc,gpc->blgp', uf, bb_re)
    x_im = jnp.einsum('blgc,gpc->blgp', uf, bb_im)
    a_shape = (1, L) + ab_re.shape
    a_re = jnp.broadcast_to(ab_re, a_shape)
    a_im = jnp.broadcast_to(ab_im, a_shape)
    _, _, h_re, h_im = lax.associative_scan(
        _complex_linear_combine, (a_re, a_im, x_re, x_im), reverse=reverse, axis=1)
    return (jnp.einsum('blgp,gcp->blgc', h_re, c_re)
            - jnp.einsum('blgp,gcp->blgc', h_im, c_im))


def _s5_mixer(u, lam_re, lam_im, log_dt, b_re, b_im, c_re, c_im, d, w_glu):
    f32 = jnp.float32
    Bn, L, W = u.shape
    uf = u.astype(f32)
    ug = uf.reshape(Bn, L, S5_GROUPS, S5_GROUP)
    y_fwd = _s5_direction(ug, lam_re[0].astype(f32), lam_im[0].astype(f32), log_dt[0].astype(f32),
                          b_re[0].astype(f32), b_im[0].astype(f32),
                          c_re[0].astype(f32), c_im[0].astype(f32), False)
    y_bwd = _s5_direction(ug, lam_re[1].astype(f32), lam_im[1].astype(f32), log_dt[1].astype(f32),
                          b_re[1].astype(f32), b_im[1].astype(f32),
                          c_re[1].astype(f32), c_im[1].astype(f32), True)
    y = (y_fwd + y_bwd).reshape(Bn, L, W) + d.astype(f32) * uf
    g = jax.nn.gelu(y).astype(u.dtype)
    return g * jax.nn.sigmoid(g @ w_glu)


def _sgu_mixer(u, v, norm_g, norm_b, w_s, b_s):
    Bn, L, W = u.shape
    u = jax.nn.gelu(u)
    v = _layernorm(jax.nn.gelu(v), norm_g, norm_b)
    vc = v.reshape(Bn, L // SGU_CHUNK, SGU_CHUNK, SGU_HEADS, SGU_HEAD_DIM)
    vs = jnp.einsum('bnjhd,hij->bnihd', vc, w_s) + b_s.T[:, :, None]
    return u * vs.reshape(Bn, L, W)


def _pool_mixer(h, pool_w, pool_scale):
    f32 = jnp.float32
    Bn, L, D = h.shape
    hf = h.astype(f32)
    cs = jnp.concatenate([jnp.zeros((Bn, 1, D), f32), jnp.cumsum(hf, axis=1)], axis=1)
    t = jnp.arange(L)
    outs = []
    for gi, w in enumerate(POOL_WINDOWS):
        lo = jnp.clip(t - w // 2, 0, L)
        hi = jnp.clip(t + w // 2, 0, L)
        csg = cs[..., gi * POOL_GROUP_DIM:(gi + 1) * POOL_GROUP_DIM]
        s = jnp.take(csg, hi, axis=1) - jnp.take(csg, lo, axis=1)
        cnt = (hi - lo).astype(f32)[None, :, None]
        pooled = s / cnt - hf[..., gi * POOL_GROUP_DIM:(gi + 1) * POOL_GROUP_DIM]
        outs.append(jnp.einsum('bld,de->ble', pooled, pool_w[gi].astype(f32)))
    y = jnp.concatenate(outs, axis=-1) * pool_scale.astype(f32)
    return y.astype(h.dtype)


def _cross_attn(h, m, w_q, w_kv, w_o):
    Bn, L, D = h.shape
    M = m.shape[1]
    q = (h @ w_q).reshape(Bn, L, CROSS_HEADS, CROSS_HEAD_DIM)
    kv = m @ w_kv
    k = kv[..., :D].reshape(Bn, M, CROSS_HEADS, CROSS_HEAD_DIM)
    v = kv[..., D:].reshape(Bn, M, CROSS_HEADS, CROSS_HEAD_DIM)
    s = jnp.einsum('blhd,bmhd->bhlm', q, k).astype(jnp.float32) * (CROSS_HEAD_DIM ** -0.5)
    p = jax.nn.softmax(s, axis=-1).astype(h.dtype)
    o = jnp.einsum('bhlm,bmhd->blhd', p, v).reshape(Bn, L, D)
    return o @ w_o


def _trunk(x, mem, p):
    for i in range(DEPTH):
        h = _rmsnorm(x, p['ffn1_norm'][i])
        x = x + 0.5 * _swiglu(h, p['ffn1_w_gate'][i], p['ffn1_w_up'][i], p['ffn1_w_down'][i])
        h = _rmsnorm(x, p['mix_norm'][i])
        if i % 2 == 0:
            e = i // 2
            z = h @ p['ab_w_in'][e]
            ua = z[..., :S5_WIDTH]
            ub = z[..., S5_WIDTH:S5_WIDTH + SGU_WIDTH]
            vb = z[..., S5_WIDTH + SGU_WIDTH:]
            ya = _s5_mixer(ua, p['s5_lambda_re'][e], p['s5_lambda_im'][e], p['s5_log_dt'][e],
                           p['s5_b_re'][e], p['s5_b_im'][e], p['s5_c_re'][e], p['s5_c_im'][e],
                           p['s5_d'][e], p['s5_w_glu'][e])
            yb = _sgu_mixer(ub, vb, p['sgu_norm_g'][e], p['sgu_norm_b'][e],
                            p['sgu_w_s'][e], p['sgu_b_s'][e])
            x = x + jnp.concatenate([ya, yb], axis=-1) @ p['ab_w_out'][e]
        else:
            o = i // 2
            x = x + _pool_mixer(h, p['pool_w'][o], p['pool_scale'][o])
        hm = _rmsnorm(mem, p['mem_norm'][i])
        h = _rmsnorm(x, p['cross_norm'][i])
        x = x + _cross_attn(h, hm, p['cross_w_q'][i], p['cross_w_kv'][i], p['cross_w_o'][i])
        h = _rmsnorm(x, p['ffn2_norm'][i])
        x = x + 0.5 * _swiglu(h, p['ffn2_w_gate'][i], p['ffn2_w_up'][i], p['ffn2_w_down'][i])
    return _rmsnorm(x, p['final_norm'])


def setup_inputs(seed: int = 0) -> dict:
    key = jax.random.key(seed)
    keys = iter(jax.random.split(key, 48))
    f32 = jnp.float32

    def nrm(shape, scale):
        return jax.random.normal(next(keys), shape, f32) * scale

    def gain(shape):
        return 1.0 + 0.01 * jax.random.normal(next(keys), shape, f32)

    D, F, G, P, C = D_MODEL, D_FF, S5_GROUPS, S5_STATE, S5_GROUP
    inp = {}
    inp['x_prompt'] = nrm((BATCH, SEQ, D), 1.0)
    inp['x_sample'] = nrm((DEC_BATCH, DEC_SEQ, D), 1.0)
    inp['mem_prompt'] = nrm((BATCH, N_MEM, D), 1.0)
    inp['mem_sample'] = nrm((DEC_BATCH, N_MEM, D), 1.0)
    inp['ffn1_norm'] = gain((DEPTH, D))
    inp['ffn1_w_gate'] = nrm((DEPTH, D, F), D ** -0.5)
    inp['ffn1_w_up'] = nrm((DEPTH, D, F), D ** -0.5)
    inp['ffn1_w_down'] = nrm((DEPTH, F, D), F ** -0.5)
    inp['mix_norm'] = gain((DEPTH, D))
    inp['ab_w_in'] = nrm((N_EVEN, D, AB_IN), D ** -0.5)
    inp['s5_lambda_re'] = -0.5 + 0.01 * jax.random.normal(next(keys), (N_EVEN, 2, G, P), f32)
    inp['s5_lambda_im'] = (np.pi * jnp.arange(P, dtype=f32)
                           + 0.01 * jax.random.normal(next(keys), (N_EVEN, 2, G, P), f32))
    inp['s5_log_dt'] = jax.random.uniform(next(keys), (N_EVEN, 2, G), f32,
                                          minval=math.log(1e-3), maxval=math.log(1e-1))
    inp['s5_b_re'] = nrm((N_EVEN, 2, G, P, C), (2 * C) ** -0.5)
    inp['s5_b_im'] = nrm((N_EVEN, 2, G, P, C), (2 * C) ** -0.5)
    inp['s5_c_re'] = nrm((N_EVEN, 2, G, C, P), P ** -0.5)
    inp['s5_c_im'] = nrm((N_EVEN, 2, G, C, P), P ** -0.5)
    inp['s5_d'] = nrm((N_EVEN, S5_WIDTH), 1.0)
    inp['s5_w_glu'] = nrm((N_EVEN, S5_WIDTH, S5_WIDTH), S5_WIDTH ** -0.5)
    inp['sgu_norm_g'] = gain((N_EVEN, SGU_WIDTH))
    inp['sgu_norm_b'] = nrm((N_EVEN, SGU_WIDTH), 0.01)
    inp['sgu_w_s'] = nrm((N_EVEN, SGU_HEADS, SGU_CHUNK, SGU_CHUNK), SGU_CHUNK ** -0.5)
    inp['sgu_b_s'] = gain((N_EVEN, SGU_HEADS, SGU_CHUNK))
    inp['ab_w_out'] = nrm((N_EVEN, AB_OUT, D), AB_OUT ** -0.5)
    inp['pool_w'] = nrm((N_ODD, POOL_GROUPS, POOL_GROUP_DIM, POOL_GROUP_DIM), POOL_GROUP_DIM ** -0.5)
    inp['pool_scale'] = gain((N_ODD, D))
    inp['cross_norm'] = gain((DEPTH, D))
    inp['mem_norm'] = gain((DEPTH, D))
    inp['cross_w_q'] = nrm((DEPTH, D, D), D ** -0.5)
    inp['cross_w_kv'] = nrm((DEPTH, D, 2 * D), D ** -0.5)
    inp['cross_w_o'] = nrm((DEPTH, D, D), D ** -0.5)
    inp['ffn2_norm'] = gain((DEPTH, D))
    inp['ffn2_w_gate'] = nrm((DEPTH, D, F), D ** -0.5)
    inp['ffn2_w_up'] = nrm((DEPTH, D, F), D ** -0.5)
    inp['ffn2_w_down'] = nrm((DEPTH, F, D), F ** -0.5)
    inp['final_norm'] = gain((D,))
    return inp


def reference(x_prompt, x_sample, mem_prompt, mem_sample,
              ffn1_norm, ffn1_w_gate, ffn1_w_up, ffn1_w_down,
              mix_norm, ab_w_in,
              s5_lambda_re, s5_lambda_im, s5_log_dt, s5_b_re, s5_b_im, s5_c_re, s5_c_im,
              s5_d, s5_w_glu,
              sgu_norm_g, sgu_norm_b, sgu_w_s, sgu_b_s, ab_w_out,
              pool_w, pool_scale,
              cross_norm, mem_norm, cross_w_q, cross_w_kv, cross_w_o,
              ffn2_norm, ffn2_w_gate, ffn2_w_up, ffn2_w_down,
              final_norm):
    p = dict(ffn1_norm=ffn1_norm, ffn1_w_gate=ffn1_w_gate, ffn1_w_up=ffn1_w_up,
             ffn1_w_down=ffn1_w_down, mix_norm=mix_norm, ab_w_in=ab_w_in,
             s5_lambda_re=s5_lambda_re, s5_lambda_im=s5_lambda_im, s5_log_dt=s5_log_dt,
             s5_b_re=s5_b_re, s5_b_im=s5_b_im, s5_c_re=s5_c_re, s5_c_im=s5_c_im,
             s5_d=s5_d, s5_w_glu=s5_w_glu,
             sgu_norm_g=sgu_norm_g, sgu_norm_b=sgu_norm_b, sgu_w_s=sgu_w_s, sgu_b_s=sgu_b_s,
             ab_w_out=ab_w_out, pool_w=pool_w, pool_scale=pool_scale,
             cross_norm=cross_norm, mem_norm=mem_norm, cross_w_q=cross_w_q,
             cross_w_kv=cross_w_kv, cross_w_o=cross_w_o,
             ffn2_norm=ffn2_norm, ffn2_w_gate=ffn2_w_gate, ffn2_w_up=ffn2_w_up,
             ffn2_w_down=ffn2_w_down, final_norm=final_norm)
    y_prompt = _trunk(x_prompt, mem_prompt, p)
    y_sample = _trunk(x_sample, mem_sample, p)
    return (y_prompt, y_sample)
```

```python
import functools
import math

import jax
import jax.numpy as jnp
from jax import lax
from jax.experimental import pallas as pl
from jax.experimental.pallas import tpu as pltpu

F32 = jnp.float32
BF16 = jnp.bfloat16

D_MODEL = 1024
D_FF = 4 * D_MODEL
N_BATCH = 8
N_MEM = 256
CROSS_HEADS = 4
CROSS_HEAD_DIM = D_MODEL // CROSS_HEADS
S5_WIDTH = D_MODEL // 2
S5_GROUP = 16
S5_GROUPS = S5_WIDTH // S5_GROUP
S5_STATE = 64
S5_NSTATE = S5_GROUPS * S5_STATE
SGU_WIDTH = D_MODEL // 2
SGU_HEADS = 4
SGU_HEAD_DIM = SGU_WIDTH // SGU_HEADS
SGU_CHUNK = 128
POOL_WINDOWS = (2, 4, 8, 16)
POOL_GROUP_DIM = D_MODEL // len(POOL_WINDOWS)
POOL_HALO = 16
EPS = 1e-6

V7X_VMEM_BYTES = 64 * 1024 * 1024
VMEM_LIMIT_BYTES = 56 * 1024 * 1024

FFN_ROWS = 512
FFN_FCHUNK = 512
TOK_ROWS = 512
S5_CHUNK = 64
S5_COLS = 512
S5_BTILE = 256
S5_CTILE = 512


def _cparams(sem):
    return pltpu.CompilerParams(dimension_semantics=sem, vmem_limit_bytes=VMEM_LIMIT_BYTES)


def _resident(shape):
    nd = len(shape)
    return pl.BlockSpec(shape, lambda *_: (0,) * nd, pipeline_mode=pl.Buffered(1))


def _rms(x, g):
    ms = jnp.mean(x * x, axis=-1, keepdims=True)
    return x * lax.rsqrt(ms + EPS) * g


def _gelu(x):
    c = math.sqrt(2.0 / math.pi)
    return x * (0.5 * (1.0 + jnp.tanh(c * (x + 0.044715 * (x * x * x)))))


def _sigmoid(x):
    return 1.0 / (1.0 + jnp.exp(-x))


def _dot(a, b):
    return jnp.dot(a, b, preferred_element_type=F32)


def _ffn_kernel(x_ref, g_ref, wg_ref, wu_ref, wd_ref, fg_ref, o_ref, *, final_norm):
    x = x_ref[...]
    h = _rms(x, g_ref[...]).astype(BF16)
    acc = jnp.zeros(x.shape, F32)
    for c in range(D_FF // FFN_FCHUNK):
        cols = slice(c * FFN_FCHUNK, (c + 1) * FFN_FCHUNK)
        gate = _dot(h, wg_ref[:, cols])
        up = _dot(h, wu_ref[:, cols])
        a = (gate * _sigmoid(gate) * up).astype(BF16)
        acc = acc + _dot(a, wd_ref[cols, :])
    y = x + 0.5 * acc
    if final_norm:
        y = _rms(y, fg_ref[...])
    o_ref[...] = y


def _ffn(x, seq_len, g, wg, wu, wd, fg, *, in_blt=False, out_blt=False, final_norm=False, name):
    tm = FFN_ROWS
    nb = N_BATCH
    if in_blt or out_blt:
        grid = (nb, seq_len // tm)
        tmaj = pl.BlockSpec((tm, D_MODEL), lambda b, i: (i, b))
        bmaj = pl.BlockSpec((None, tm, D_MODEL), lambda b, i: (b, i, 0))
        x_spec = bmaj if in_blt else tmaj
        o_spec = bmaj if out_blt else tmaj
        sem = ("arbitrary", "arbitrary")
    else:
        x = x.reshape(seq_len * nb, D_MODEL)
        grid = (seq_len * nb // tm,)
        x_spec = o_spec = pl.BlockSpec((tm, D_MODEL), lambda i: (i, 0))
        sem = ("arbitrary",)
    out_shape = (nb, seq_len, D_MODEL) if out_blt else (
        (seq_len, nb * D_MODEL) if in_blt else (seq_len * nb, D_MODEL))
    out = pl.pallas_call(
        functools.partial(_ffn_kernel, final_norm=final_norm),
        grid=grid,
        in_specs=[x_spec, _resident((1, D_MODEL)), _resident((D_MODEL, D_FF)),
                  _resident((D_MODEL, D_FF)), _resident((D_FF, D_MODEL)), _resident((1, D_MODEL))],
        out_specs=o_spec,
        out_shape=jax.ShapeDtypeStruct(out_shape, F32),
        compiler_params=_cparams(sem),
        name=name,
    )(x, g, wg, wu, wd, fg)
    return out if out_blt else out.reshape(seq_len, nb * D_MODEL)


def _inproj_sgu_kernel(x_ref, g_ref, win_ref, lng_ref, lnb_ref, ws_ref, bs_ref, ua_ref, yb_ref):
    h = _rms(x_ref[...], g_ref[...]).astype(BF16)
    z = _dot(h, win_ref[...])
    ua_ref[...] = z[:, :S5_WIDTH]
    u = _gelu(z[:, S5_WIDTH:S5_WIDTH + SGU_WIDTH])
    v = _gelu(z[:, S5_WIDTH + SGU_WIDTH:])
    mu = jnp.mean(v, axis=-1, keepdims=True)
    vc = v - mu
    var = jnp.mean(vc * vc, axis=-1, keepdims=True)
    v = (vc * lax.rsqrt(var + EPS) * lng_ref[...] + lnb_ref[...]).astype(BF16)
    for n in range(x_ref.shape[0] // SGU_CHUNK):
        rows = slice(n * SGU_CHUNK, (n + 1) * SGU_CHUNK)
        for hd in range(SGU_HEADS):
            cols = slice(hd * SGU_HEAD_DIM, (hd + 1) * SGU_HEAD_DIM)
            vs = _dot(ws_ref[hd], v[rows, cols]) + bs_ref[hd]
            yb_ref[rows, cols] = (u[rows, cols] * vs).astype(BF16)


def _inproj_sgu(x, seq_len, g, win, lng, lnb, ws, bs):
    tm = TOK_ROWS
    grid = (N_BATCH, seq_len // tm)
    tok = lambda w: pl.BlockSpec((tm, w), lambda b, i: (i, b))
    return pl.pallas_call(
        _inproj_sgu_kernel,
        grid=grid,
        in_specs=[tok(D_MODEL), _resident((1, D_MODEL)), _resident(win.shape),
                  _resident((1, SGU_WIDTH)), _resident((1, SGU_WIDTH)),
                  _resident(ws.shape), _resident(bs.shape)],
        out_specs=[tok(S5_WIDTH), tok(SGU_WIDTH)],
        out_shape=[jax.ShapeDtypeStruct((seq_len, N_BATCH * S5_WIDTH), F32),
                   jax.ShapeDtypeStruct((seq_len, N_BATCH * SGU_WIDTH), BF16)],
        compiler_params=_cparams(("arbitrary", "arbitrary")),
        name="inproj_sgu",
    )(x, g, win, lng, lnb, ws, bs)


def _s5_disc_kernel(lre_ref, lim_ref, ldt_ref, bre_ref, bim_ref, are_ref, aim_ref, bbre_ref, bbim_ref):
    lam_re = lre_ref[...]
    lam_im = lim_ref[...]
    dt = jnp.exp(ldt_ref[...])
    mag = jnp.exp(lam_re * dt)
    ang = lam_im * dt
    ab_re = mag * jnp.cos(ang)
    ab_im = mag * jnp.sin(ang)
    are_ref[...] = ab_re
    aim_ref[...] = ab_im
    nr = ab_re - 1.0
    ni = ab_im
    den = lam_re * lam_re + lam_im * lam_im
    q_re = ((nr * lam_re + ni * lam_im) / den)[:, None, :]
    q_im = ((ni * lam_re - nr * lam_im) / den)[:, None, :]
    b_re = bre_ref[...]
    b_im = bim_ref[...]
    bbre_ref[...] = q_re * b_re - q_im * b_im
    bbim_ref[...] = q_re * b_im + q_im * b_re


def _s5_discretise(lam_re, lam_im, log_dt, b_re, b_im):
    g2 = 2 * S5_GROUPS
    flat = lambda a: a.reshape(g2, S5_STATE)
    bt = lambda a: jnp.swapaxes(a, -1, -2).reshape(g2, S5_GROUP, S5_STATE)
    outs = pl.pallas_call(
        _s5_disc_kernel,
        out_shape=[jax.ShapeDtypeStruct((g2, S5_STATE), F32)] * 2
        + [jax.ShapeDtypeStruct((g2, S5_GROUP, S5_STATE), F32)] * 2,
        name="s5_discretise",
    )(flat(lam_re), flat(lam_im), log_dt.reshape(g2, 1), bt(b_re), bt(b_im))
    a_re, a_im, bb_re, bb_im = outs
    shape4 = (2, S5_GROUPS, S5_GROUP, S5_STATE)
    return (a_re.reshape(2, S5_NSTATE), a_im.reshape(2, S5_NSTATE),
            bb_re.reshape(shape4), bb_im.reshape(shape4))


def _s5_block_weights(bb_re, bb_im, c_re, c_im):
    eye = jnp.eye(S5_GROUPS, dtype=F32)
    big_b = lambda w: jnp.einsum('dgcp,gh->dgchp', w, eye).reshape(2, S5_WIDTH, S5_NSTATE)
    big_c = lambda w: jnp.einsum('dgcp,gh->dgphc', w, eye).reshape(2, S5_NSTATE, S5_WIDTH)
    nbt = S5_NSTATE // S5_BTILE
    ch_per_btile = S5_BTILE // S5_STATE * S5_GROUP
    def cut_b(w):
        tiles = []
        for j in range(nbt):
            k0 = (j * ch_per_btile) // 128 * 128
            tiles.append(w[:, k0:k0 + 128, j * S5_BTILE:(j + 1) * S5_BTILE])
        return jnp.stack(tiles, axis=1)
    nct = S5_NSTATE // S5_CTILE
    ch_per_ctile = S5_CTILE // S5_STATE * S5_GROUP
    def cut_c(w):
        return jnp.stack([w[:, m * S5_CTILE:(m + 1) * S5_CTILE, m * ch_per_ctile:(m + 1) * ch_per_ctile]
                          for m in range(nct)], axis=1)
    wb = jnp.stack([cut_b(big_b(bb_re)), cut_b(big_b(bb_im))], axis=1).astype(BF16)
    wc = jnp.stack([cut_c(big_c(c_re)), cut_c(big_c(-c_im))], axis=1).astype(BF16)
    return wb, wc


def _s5_scan_kernel(uf_ref, ub_ref, wb_ref, wc_ref, are_ref, aim_ref, yf_ref, yb_ref,
                    xre_ref, xim_ref, st_ref):
    lc = S5_CHUNK

    @pl.when(pl.program_id(0) == 0)
    def _():
        st_ref[...] = jnp.zeros(st_ref.shape, F32)

    ch_per_btile = S5_BTILE // S5_STATE * S5_GROUP
    for d, (u_ref, y_ref) in enumerate(((uf_ref, yf_ref), (ub_ref, yb_ref))):
        u = u_ref[...].astype(BF16)
        for j in range(S5_NSTATE // S5_BTILE):
            k0 = (j * ch_per_btile) // 128 * 128
            cols = slice(j * S5_BTILE, (j + 1) * S5_BTILE)
            lhs = u[:, k0:k0 + 128]
            xre_ref[:, cols] = _dot(lhs, wb_ref[d, 0, j])
            xim_ref[:, cols] = _dot(lhs, wb_ref[d, 1, j])

        for cb in range(S5_NSTATE // S5_COLS):
            cols = slice(cb * S5_COLS, (cb + 1) * S5_COLS)
            a_re = jnp.broadcast_to(are_ref[d:d + 1, cols], (N_BATCH, S5_COLS))
            a_im = jnp.broadcast_to(aim_ref[d:d + 1, cols], (N_BATCH, S5_COLS))

            def step(s, carry, cols=cols, a_re=a_re, a_im=a_im, d=d):
                h_re, h_im = carry
                t = s if d == 0 else lc - 1 - s
                rows = pl.ds(pl.multiple_of(t * N_BATCH, N_BATCH), N_BATCH)
                n_re = a_re * h_re - a_im * h_im + xre_ref[rows, cols]
                n_im = a_re * h_im + a_im * h_re + xim_ref[rows, cols]
                xre_ref[rows, cols] = n_re
                xim_ref[rows, cols] = n_im
                return n_re, n_im

            h_re, h_im = lax.fori_loop(0, lc, step, (st_ref[d, 0, :, cols], st_ref[d, 1, :, cols]),
                                       unroll=8)
            st_ref[d, 0, :, cols] = h_re
            st_ref[d, 1, :, cols] = h_im

        for m in range(S5_NSTATE // S5_CTILE):
            rows = slice(m * S5_CTILE, (m + 1) * S5_CTILE)
            y_ref[:, m * 128:(m + 1) * 128] = (
                _dot(xre_ref[:, rows].astype(BF16), wc_ref[d, 0, m])
                + _dot(xim_ref[:, rows].astype(BF16), wc_ref[d, 1, m]))


def _s5_scan(ua, seq_len, wb, wc, a_re, a_im):
    rows = S5_CHUNK * N_BATCH
    nc = seq_len // S5_CHUNK
    ua = ua.reshape(seq_len * N_BATCH, S5_WIDTH)
    fwd = pl.BlockSpec((rows, S5_WIDTH), lambda c: (c, 0))
    bwd = pl.BlockSpec((rows, S5_WIDTH), lambda c: (nc - 1 - c, 0))
    yf, yb = pl.pallas_call(
        _s5_scan_kernel,
        grid=(nc,),
        in_specs=[fwd, bwd, _resident(wb.shape), _resident(wc.shape),
                  _resident(a_re.shape), _resident(a_im.shape)],
        out_specs=[fwd, bwd],
        out_shape=[jax.ShapeDtypeStruct((seq_len * N_BATCH, S5_WIDTH), F32)] * 2,
        scratch_shapes=[pltpu.VMEM((rows, S5_NSTATE), F32), pltpu.VMEM((rows, S5_NSTATE), F32),
                        pltpu.VMEM((2, 2, N_BATCH, S5_NSTATE), F32)],
        compiler_params=_cparams(("arbitrary",)),
        name="s5_scan",
    )(ua, ua, wb, wc, a_re, a_im)
    return yf, yb


def _mix_out_kernel(x_ref, yf_ref, ybk_ref, ua_ref, sg_ref, d_ref, wglu_ref, woa_ref, wob_ref, o_ref):
    y = yf_ref[...] + ybk_ref[...] + d_ref[...] * ua_ref[...]
    g = _gelu(y)
    ya = g * _sigmoid(_dot(g.astype(BF16), wglu_ref[...]))
    o_ref[...] = x_ref[...] + _dot(ya.astype(BF16), woa_ref[...]) + _dot(sg_ref[...], wob_ref[...])


def _mix_out(x, seq_len, yf, ybk, ua, sg, d, wglu, woa, wob):
    tm = TOK_ROWS
    n = seq_len * N_BATCH
    row = lambda w: pl.BlockSpec((tm, w), lambda i: (i, 0))
    out = pl.pallas_call(
        _mix_out_kernel,
        grid=(n // tm,),
        in_specs=[row(D_MODEL), row(S5_WIDTH), row(S5_WIDTH), row(S5_WIDTH), row(SGU_WIDTH),
                  _resident((1, S5_WIDTH)), _resident(wglu.shape), _resident(woa.shape),
                  _resident(wob.shape)],
        out_specs=row(D_MODEL),
        out_shape=jax.ShapeDtypeStruct((n, D_MODEL), F32),
        compiler_params=_cparams(("arbitrary",)),
        name="mix_out",
    )(x.reshape(n, D_MODEL), yf, ybk, ua.reshape(n, S5_WIDTH), sg.reshape(n, SGU_WIDTH),
      d, wglu, woa, wob)
    return out.reshape(seq_len, N_BATCH * D_MODEL)


def _pool_kernel(xc_ref, xp_ref, xn_ref, g_ref, pw_ref, ps_ref, o_ref, hp_ref, *, seq_len):
    tm = xc_ref.shape[0]
    i = pl.program_id(1)
    g = g_ref[...]
    x = xc_ref[...]
    h = _rms(x, g)
    keep_prev = (i > 0).astype(F32)
    keep_next = (i < pl.num_programs(1) - 1).astype(F32)
    hp_ref[0:POOL_HALO, :] = _rms(xp_ref[...], g) * keep_prev
    hp_ref[POOL_HALO:POOL_HALO + tm, :] = h
    hp_ref[POOL_HALO + tm:, :] = _rms(xn_ref[...], g) * keep_next
    t = i * tm + lax.broadcasted_iota(jnp.int32, (tm, 1), 0)
    for gi, w in enumerate(POOL_WINDOWS):
        cols = slice(gi * POOL_GROUP_DIM, (gi + 1) * POOL_GROUP_DIM)
        s = hp_ref[POOL_HALO - w // 2:POOL_HALO - w // 2 + tm, cols]
        for k in range(1 - w // 2, w // 2):
            s = s + hp_ref[POOL_HALO + k:POOL_HALO + k + tm, cols]
        lo = jnp.clip(t - w // 2, 0, seq_len)
        hi = jnp.clip(t + w // 2, 0, seq_len)
        cnt = (hi - lo).astype(F32)
        pooled = s / cnt - h[:, cols]
        y = _dot(pooled.astype(BF16), pw_ref[gi]) * ps_ref[:, cols]
        o_ref[:, cols] = x[:, cols] + y


def _pool(x, seq_len, g, pw, ps):
    tm = TOK_ROWS
    nt = seq_len // tm
    per = tm // POOL_HALO
    nh = seq_len // POOL_HALO
    cur = pl.BlockSpec((tm, D_MODEL), lambda b, i: (i, b))
    prev = pl.BlockSpec((POOL_HALO, D_MODEL), lambda b, i: (jnp.maximum(i * per - 1, 0), b))
    nxt = pl.BlockSpec((POOL_HALO, D_MODEL), lambda b, i: (jnp.minimum((i + 1) * per, nh - 1), b))
    return pl.pallas_call(
        functools.partial(_pool_kernel, seq_len=seq_len),
        grid=(N_BATCH, nt),
        in_specs=[cur, prev, nxt, _resident((1, D_MODEL)), _resident(pw.shape), _resident((1, D_MODEL))],
        out_specs=cur,
        out_shape=jax.ShapeDtypeStruct((seq_len, N_BATCH * D_MODEL), F32),
        scratch_shapes=[pltpu.VMEM((tm + 2 * POOL_HALO, D_MODEL), F32)],
        compiler_params=_cparams(("arbitrary", "arbitrary")),
        name="pool_mixer",
    )(x, x, x, g, pw, ps)


def _kv_kernel(m_ref, g_ref, wkv_ref, kt_ref, v_ref):
    hm = _rms(m_ref[...], g_ref[...]).astype(BF16)
    kv = _dot(hm, wkv_ref[...])
    kt_ref[...] = kv[:, :D_MODEL].T.astype(BF16)
    v_ref[...] = kv[:, D_MODEL:].astype(BF16)


def _kv(mem, g, wkv):
    return pl.pallas_call(
        _kv_kernel,
        grid=(N_BATCH,),
        in_specs=[pl.BlockSpec((None, N_MEM, D_MODEL), lambda b: (b, 0, 0)),
                  _resident((1, D_MODEL)), _resident(wkv.shape)],
        out_specs=[pl.BlockSpec((None, D_MODEL, N_MEM), lambda b: (b, 0, 0)),
                   pl.BlockSpec((None, N_MEM, D_MODEL), lambda b: (b, 0, 0))],
        out_shape=[jax.ShapeDtypeStruct((N_BATCH, D_MODEL, N_MEM), BF16),
                   jax.ShapeDtypeStruct((N_BATCH, N_MEM, D_MODEL), BF16)],
        compiler_params=_cparams(("arbitrary",)),
        name="mem_kv",
    )(mem, g, wkv)


def _cross_kernel(x_ref, g_ref, wq_ref, kt_ref, v_ref, wo_ref, o_ref):
    x = x_ref[...]
    h = _rms(x, g_ref[...]).astype(BF16)
    q = (_dot(h, wq_ref[...]) * (CROSS_HEAD_DIM ** -0.5)).astype(BF16)
    heads = []
    for hd in range(CROSS_HEADS):
        cols = slice(hd * CROSS_HEAD_DIM, (hd + 1) * CROSS_HEAD_DIM)
        s = _dot(q[:, cols], kt_ref[cols, :])
        e = jnp.exp(s - jnp.max(s, axis=-1, keepdims=True))
        p = e / jnp.sum(e, axis=-1, keepdims=True)
        heads.append(_dot(p.astype(BF16), v_ref[:, cols]).astype(BF16))
    o = jnp.concatenate(heads, axis=-1)
    o_ref[...] = x + _dot(o, wo_ref[...])


def _cross(x, seq_len, g, wq, kt, v, wo):
    tm = TOK_ROWS
    tok = pl.BlockSpec((tm, D_MODEL), lambda b, i: (i, b))
    return pl.pallas_call(
        _cross_kernel,
        grid=(N_BATCH, seq_len // tm),
        in_specs=[tok, _resident((1, D_MODEL)), _resident(wq.shape),
                  pl.BlockSpec((None, D_MODEL, N_MEM), lambda b, i: (b, 0, 0)),
                  pl.BlockSpec((None, N_MEM, D_MODEL), lambda b, i: (b, 0, 0)),
                  _resident(wo.shape)],
        out_specs=tok,
        out_shape=jax.ShapeDtypeStruct((seq_len, N_BATCH * D_MODEL), F32),
        compiler_params=_cparams(("arbitrary", "arbitrary")),
        name="cross_attn",
    )(x, g, wq, kt, v, wo)


def _trunk(x, mem, p):
    seq_len = x.shape[1]
    row = lambda a: a.reshape(1, -1)
    fg = row(p['final_norm'])
    for i in range(2):
        x = _ffn(x, seq_len, row(p['ffn1_norm'][i]), p['ffn1_w_gate'][i], p['ffn1_w_up'][i],
                 p['ffn1_w_down'][i], fg, in_blt=(i == 0), name=f"ffn1_l{i}")
        mg = row(p['mix_norm'][i])
        if i == 0:
            ua, sg = _inproj_sgu(x, seq_len, mg, p['ab_w_in'], row(p['sgu_norm_g']), row(p['sgu_norm_b']),
                                 p['sgu_w_s'], p['sgu_b_s'])
            yf, ybk = _s5_scan(ua, seq_len, p['s5_wb'], p['s5_wc'], p['s5_a_re'], p['s5_a_im'])
            x = _mix_out(x, seq_len, yf, ybk, ua, sg, row(p['s5_d']), p['s5_w_glu'],
                         p['ab_w_out_a'], p['ab_w_out_b'])
        else:
            x = _pool(x, seq_len, mg, p['pool_w'], row(p['pool_scale']))
        kt, v = _kv(mem, row(p['mem_norm'][i]), p['cross_w_kv'][i])
        x = _cross(x, seq_len, row(p['cross_norm'][i]), p['cross_w_q'][i], kt, v, p['cross_w_o'][i])
        x = _ffn(x, seq_len, row(p['ffn2_norm'][i]), p['ffn2_w_gate'][i], p['ffn2_w_up'][i],
                 p['ffn2_w_down'][i], fg, out_blt=(i == 1), final_norm=(i == 1), name=f"ffn2_l{i}")
    return x


def kernel(x_prompt, x_sample, mem_prompt, mem_sample, ffn1_norm, ffn1_w_gate, ffn1_w_up, ffn1_w_down, mix_norm, ab_w_in, s5_lambda_re, s5_lambda_im, s5_log_dt, s5_b_re, s5_b_im, s5_c_re, s5_c_im, s5_d, s5_w_glu, sgu_norm_g, sgu_norm_b, sgu_w_s, sgu_b_s, ab_w_out, pool_w, pool_scale, cross_norm, mem_norm, cross_w_q, cross_w_kv, cross_w_o, ffn2_norm, ffn2_w_gate, ffn2_w_up, ffn2_w_down, final_norm):
    bf = lambda a: a.astype(BF16)
    a_re, a_im, bb_re, bb_im = _s5_discretise(s5_lambda_re[0], s5_lambda_im[0], s5_log_dt[0],
                                              s5_b_re[0], s5_b_im[0])
    wb, wc = _s5_block_weights(bb_re, bb_im, s5_c_re[0], s5_c_im[0])
    p = dict(
        ffn1_norm=ffn1_norm, ffn1_w_gate=bf(ffn1_w_gate), ffn1_w_up=bf(ffn1_w_up),
        ffn1_w_down=bf(ffn1_w_down), mix_norm=mix_norm, ab_w_in=bf(ab_w_in[0]),
        s5_wb=wb, s5_wc=wc, s5_a_re=a_re, s5_a_im=a_im, s5_d=s5_d[0], s5_w_glu=bf(s5_w_glu[0]),
        sgu_norm_g=sgu_norm_g[0], sgu_norm_b=sgu_norm_b[0], sgu_w_s=bf(sgu_w_s[0]),
        sgu_b_s=jnp.broadcast_to(sgu_b_s[0][:, :, None], (SGU_HEADS, SGU_CHUNK, SGU_HEAD_DIM)),
        ab_w_out_a=bf(ab_w_out[0, :S5_WIDTH]), ab_w_out_b=bf(ab_w_out[0, S5_WIDTH:]),
        pool_w=bf(pool_w[0]), pool_scale=pool_scale[0],
        cross_norm=cross_norm, mem_norm=mem_norm, cross_w_q=bf(cross_w_q), cross_w_kv=bf(cross_w_kv),
        cross_w_o=bf(cross_w_o), ffn2_norm=ffn2_norm, ffn2_w_gate=bf(ffn2_w_gate),
        ffn2_w_up=bf(ffn2_w_up), ffn2_w_down=bf(ffn2_w_down), final_norm=final_norm)
    return (_trunk(x_prompt, mem_prompt, p), _trunk(x_sample, mem_sample, p))
```

```python
import functools
import math

import jax
import jax.numpy as jnp
from jax import lax
from jax.experimental import pallas as pl
from jax.experimental.pallas import tpu as pltpu

F32 = jnp.float32
BF16 = jnp.bfloat16

D_MODEL = 1024
D_FF = 4 * D_MODEL
N_BATCH = 8
N_MEM = 256
CROSS_HEADS = 4
CROSS_HEAD_DIM = D_MODEL // CROSS_HEADS
S5_WIDTH = D_MODEL // 2
S5_GROUP = 16
S5_GROUPS = S5_WIDTH // S5_GROUP
S5_STATE = 64
S5_NSTATE = S5_GROUPS * S5_STATE
SGU_WIDTH = D_MODEL // 2
SGU_HEADS = 4
SGU_HEAD_DIM = SGU_WIDTH // SGU_HEADS
SGU_CHUNK = 128
POOL_WINDOWS = (2, 4, 8, 16)
POOL_GROUP_DIM = D_MODEL // len(POOL_WINDOWS)
POOL_HALO = 16
EPS = 1e-6

V7X_VMEM_BYTES = 64 * 1024 * 1024
VMEM_LIMIT_BYTES = 56 * 1024 * 1024

FFN_ROWS = 512
FFN_FCHUNK = 512
TOK_ROWS = 512
S5_CHUNK = 64
S5_COLS = 512
S5_BTILE = 256
S5_CTILE = 512


def _cparams(sem):
    return pltpu.CompilerParams(dimension_semantics=sem, vmem_limit_bytes=VMEM_LIMIT_BYTES)


def _resident(shape):
    nd = len(shape)
    return pl.BlockSpec(shape, lambda *_: (0,) * nd, pipeline_mode=pl.Buffered(1))


def _rms(x, g):
    ms = jnp.mean(x * x, axis=-1, keepdims=True)
    return x * lax.rsqrt(ms + EPS) * g


def _gelu(x):
    c = math.sqrt(2.0 / math.pi)
    return x * (0.5 * (1.0 + jnp.tanh(c * (x + 0.044715 * (x * x * x)))))


def _sigmoid(x):
    return 1.0 / (1.0 + jnp.exp(-x))


def _dot(a, b):
    return jnp.dot(a, b, preferred_element_type=F32)


def _ffn_kernel(x_ref, g_ref, wg_ref, wu_ref, wd_ref, fg_ref, o_ref, *, final_norm):
    x = x_ref[...]
    h = _rms(x, g_ref[...]).astype(BF16)
    acc = jnp.zeros(x.shape, F32)
    for c in range(D_FF // FFN_FCHUNK):
        cols = slice(c * FFN_FCHUNK, (c + 1) * FFN_FCHUNK)
        gate = _dot(h, wg_ref[:, cols])
        up = _dot(h, wu_ref[:, cols])
        a = (gate * _sigmoid(gate) * up).astype(BF16)
        acc = acc + _dot(a, wd_ref[cols, :])
    y = x + 0.5 * acc
    if final_norm:
        y = _rms(y, fg_ref[...])
    o_ref[...] = y


def _row_spec(tm, width):
    return pl.BlockSpec((tm, width), lambda i: (i, 0))


def _ffn(x, g, wg, wu, wd, fg, *, final_norm=False, name):
    tm = FFN_ROWS
    n = x.shape[0]
    return pl.pallas_call(
        functools.partial(_ffn_kernel, final_norm=final_norm),
        grid=(n // tm,),
        in_specs=[_row_spec(tm, D_MODEL), _resident((1, D_MODEL)), _resident((D_MODEL, D_FF)),
                  _resident((D_MODEL, D_FF)), _resident((D_FF, D_MODEL)), _resident((1, D_MODEL))],
        out_specs=_row_spec(tm, D_MODEL),
        out_shape=jax.ShapeDtypeStruct((n, D_MODEL), F32),
        compiler_params=_cparams(("arbitrary",)),
        name=name,
    )(x, g, wg, wu, wd, fg)


def _inproj_sgu_kernel(x_ref, g_ref, win_ref, lng_ref, lnb_ref, ws_ref, bs_ref, ua_ref, yb_ref):
    h = _rms(x_ref[...], g_ref[...]).astype(BF16)
    z = _dot(h, win_ref[...])
    ua_ref[...] = z[:, :S5_WIDTH]
    u = _gelu(z[:, S5_WIDTH:S5_WIDTH + SGU_WIDTH])
    v = _gelu(z[:, S5_WIDTH + SGU_WIDTH:])
    mu = jnp.mean(v, axis=-1, keepdims=True)
    vc = v - mu
    var = jnp.mean(vc * vc, axis=-1, keepdims=True)
    v = (vc * lax.rsqrt(var + EPS) * lng_ref[...] + lnb_ref[...]).astype(BF16)
    for n in range(x_ref.shape[0] // SGU_CHUNK):
        rows = slice(n * SGU_CHUNK, (n + 1) * SGU_CHUNK)
        for hd in range(SGU_HEADS):
            cols = slice(hd * SGU_HEAD_DIM, (hd + 1) * SGU_HEAD_DIM)
            vs = _dot(ws_ref[hd], v[rows, cols]) + bs_ref[hd]
            yb_ref[rows, cols] = (u[rows, cols] * vs).astype(BF16)


def _inproj_sgu(x, g, win, lng, lnb, ws, bs):
    tm = TOK_ROWS
    n = x.shape[0]
    return pl.pallas_call(
        _inproj_sgu_kernel,
        grid=(n // tm,),
        in_specs=[_row_spec(tm, D_MODEL), _resident((1, D_MODEL)), _resident(win.shape),
                  _resident((1, SGU_WIDTH)), _resident((1, SGU_WIDTH)),
                  _resident(ws.shape), _resident(bs.shape)],
        out_specs=[_row_spec(tm, S5_WIDTH), _row_spec(tm, SGU_WIDTH)],
        out_shape=[jax.ShapeDtypeStruct((n, S5_WIDTH), F32),
                   jax.ShapeDtypeStruct((n, SGU_WIDTH), BF16)],
        compiler_params=_cparams(("arbitrary",)),
        name="inproj_sgu",
    )(x, g, win, lng, lnb, ws, bs)


def _s5_disc_kernel(lre_ref, lim_ref, ldt_ref, bre_ref, bim_ref, are_ref, aim_ref, bbre_ref, bbim_ref):
    lam_re = lre_ref[...]
    lam_im = lim_ref[...]
    dt = jnp.exp(ldt_ref[...])
    mag = jnp.exp(lam_re * dt)
    ang = lam_im * dt
    ab_re = mag * jnp.cos(ang)
    ab_im = mag * jnp.sin(ang)
    are_ref[...] = ab_re
    aim_ref[...] = ab_im
    nr = ab_re - 1.0
    ni = ab_im
    den = lam_re * lam_re + lam_im * lam_im
    q_re = ((nr * lam_re + ni * lam_im) / den)[:, None, :]
    q_im = ((ni * lam_re - nr * lam_im) / den)[:, None, :]
    b_re = bre_ref[...]
    b_im = bim_ref[...]
    bbre_ref[...] = q_re * b_re - q_im * b_im
    bbim_ref[...] = q_re * b_im + q_im * b_re


def _s5_discretise(lam_re, lam_im, log_dt, b_re, b_im):
    g2 = 2 * S5_GROUPS
    flat = lambda a: a.reshape(g2, S5_STATE)
    bt = lambda a: jnp.swapaxes(a, -1, -2).reshape(g2, S5_GROUP, S5_STATE)
    outs = pl.pallas_call(
        _s5_disc_kernel,
        out_shape=[jax.ShapeDtypeStruct((g2, S5_STATE), F32)] * 2
        + [jax.ShapeDtypeStruct((g2, S5_GROUP, S5_STATE), F32)] * 2,
        name="s5_discretise",
    )(flat(lam_re), flat(lam_im), log_dt.reshape(g2, 1), bt(b_re), bt(b_im))
    a_re, a_im, bb_re, bb_im = outs
    shape4 = (2, S5_GROUPS, S5_GROUP, S5_STATE)
    return (a_re.reshape(2, S5_NSTATE), a_im.reshape(2, S5_NSTATE),
            bb_re.reshape(shape4), bb_im.reshape(shape4))


def _s5_block_weights(bb_re, bb_im, c_re, c_im):
    eye = jnp.eye(S5_GROUPS, dtype=F32)
    big_b = lambda w: jnp.einsum('dgcp,gh->dgchp', w, eye).reshape(2, S5_WIDTH, S5_NSTATE)
    big_c = lambda w: jnp.einsum('dgcp,gh->dgphc', w, eye).reshape(2, S5_NSTATE, S5_WIDTH)
    nbt = S5_NSTATE // S5_BTILE
    ch_per_btile = S5_BTILE // S5_STATE * S5_GROUP
    def cut_b(w):
        tiles = []
        for j in range(nbt):
            k0 = (j * ch_per_btile) // 128 * 128
            tiles.append(w[:, k0:k0 + 128, j * S5_BTILE:(j + 1) * S5_BTILE])
        return jnp.stack(tiles, axis=1)
    nct = S5_NSTATE // S5_CTILE
    ch_per_ctile = S5_CTILE // S5_STATE * S5_GROUP
    def cut_c(w):
        return jnp.stack([w[:, m * S5_CTILE:(m + 1) * S5_CTILE, m * ch_per_ctile:(m + 1) * ch_per_ctile]
                          for m in range(nct)], axis=1)
    wb = jnp.stack([cut_b(big_b(bb_re)), cut_b(big_b(bb_im))], axis=1).astype(BF16)
    wc = jnp.stack([cut_c(big_c(c_re)), cut_c(big_c(-c_im))], axis=1).astype(BF16)
    return wb, wc


def _s5_scan_kernel(uf_ref, ub_ref, perm_ref, permt_ref, wb_ref, wc_ref, are_ref, aim_ref, yf_ref, yb_ref,
                    xre_ref, xim_ref, st_ref):
    lc = S5_CHUNK
    rows_per_step = lc * N_BATCH

    @pl.when(pl.program_id(0) == 0)
    def _():
        st_ref[...] = jnp.zeros(st_ref.shape, F32)

    ch_per_btile = S5_BTILE // S5_STATE * S5_GROUP
    for d, (u_ref, y_ref) in enumerate(((uf_ref, yf_ref), (ub_ref, yb_ref))):
        u = _dot(perm_ref[...], u_ref[...].reshape(rows_per_step, S5_WIDTH).astype(BF16)).astype(BF16)
        for j in range(S5_NSTATE // S5_BTILE):
            k0 = (j * ch_per_btile) // 128 * 128
            cols = slice(j * S5_BTILE, (j + 1) * S5_BTILE)
            lhs = u[:, k0:k0 + 128]
            xre_ref[:, cols] = _dot(lhs, wb_ref[d, 0, j])
            xim_ref[:, cols] = _dot(lhs, wb_ref[d, 1, j])

        for cb in range(S5_NSTATE // S5_COLS):
            cols = slice(cb * S5_COLS, (cb + 1) * S5_COLS)
            a_re = jnp.broadcast_to(are_ref[d:d + 1, cols], (N_BATCH, S5_COLS))
            a_im = jnp.broadcast_to(aim_ref[d:d + 1, cols], (N_BATCH, S5_COLS))

            def step(s, carry, cols=cols, a_re=a_re, a_im=a_im, d=d):
                h_re, h_im = carry
                t = s if d == 0 else lc - 1 - s
                rows = pl.ds(pl.multiple_of(t * N_BATCH, N_BATCH), N_BATCH)
                n_re = a_re * h_re - a_im * h_im + xre_ref[rows, cols]
                n_im = a_re * h_im + a_im * h_re + xim_ref[rows, cols]
                xre_ref[rows, cols] = n_re
                xim_ref[rows, cols] = n_im
                return n_re, n_im

            h_re, h_im = lax.fori_loop(0, lc, step, (st_ref[d, 0, :, cols], st_ref[d, 1, :, cols]),
                                       unroll=8)
            st_ref[d, 0, :, cols] = h_re
            st_ref[d, 1, :, cols] = h_im

        parts = []
        for m in range(S5_NSTATE // S5_CTILE):
            rows = slice(m * S5_CTILE, (m + 1) * S5_CTILE)
            parts.append(_dot(xre_ref[:, rows].astype(BF16), wc_ref[d, 0, m])
                         + _dot(xim_ref[:, rows].astype(BF16), wc_ref[d, 1, m]))
        y = jnp.concatenate(parts, axis=-1)
        y_hi = y.astype(BF16)
        y_lo = (y - y_hi.astype(F32)).astype(BF16)
        y = _dot(permt_ref[...], y_hi) + _dot(permt_ref[...], y_lo)
        y_ref[...] = y.reshape(N_BATCH, lc, S5_WIDTH)


def _time_major_permutation():
    r = jnp.arange(S5_CHUNK * N_BATCH)
    src = (r % N_BATCH) * S5_CHUNK + r // N_BATCH
    return jax.nn.one_hot(src, S5_CHUNK * N_BATCH, dtype=BF16)


def _s5_scan(ua, seq_len, wb, wc, a_re, a_im):
    nc = seq_len // S5_CHUNK
    rows = S5_CHUNK * N_BATCH
    ua = ua.reshape(N_BATCH, seq_len, S5_WIDTH)
    perm = _time_major_permutation()
    fwd = pl.BlockSpec((N_BATCH, S5_CHUNK, S5_WIDTH), lambda c: (0, c, 0))
    bwd = pl.BlockSpec((N_BATCH, S5_CHUNK, S5_WIDTH), lambda c: (0, nc - 1 - c, 0))
    yf, yb = pl.pallas_call(
        _s5_scan_kernel,
        grid=(nc,),
        in_specs=[fwd, bwd, _resident(perm.shape), _resident(perm.shape), _resident(wb.shape),
                  _resident(wc.shape), _resident(a_re.shape), _resident(a_im.shape)],
        out_specs=[fwd, bwd],
        out_shape=[jax.ShapeDtypeStruct((N_BATCH, seq_len, S5_WIDTH), F32)] * 2,
        scratch_shapes=[pltpu.VMEM((rows, S5_NSTATE), F32), pltpu.VMEM((rows, S5_NSTATE), F32),
                        pltpu.VMEM((2, 2, N_BATCH, S5_NSTATE), F32)],
        compiler_params=_cparams(("arbitrary",)),
        name="s5_scan",
    )(ua, ua, perm, perm.T, wb, wc, a_re, a_im)
    n = N_BATCH * seq_len
    return yf.reshape(n, S5_WIDTH), yb.reshape(n, S5_WIDTH)


def _mix_out(x, yf_ref, ybk_ref, ua_ref, sg_ref, d_ref, wglu_ref, woa_ref, wob_ref):
    y = yf_ref[...] + ybk_ref[...] + d_ref[...] * ua_ref[...]
    g = _gelu(y)
    ya = g * _sigmoid(_dot(g.astype(BF16), wglu_ref[...]))
    return x + _dot(ya.astype(BF16), woa_ref[...]) + _dot(sg_ref[...], wob_ref[...])


def _pool(x, xp_ref, xn_ref, g_ref, pw_ref, ps_ref, hp_ref, *, seq_len):
    tm = x.shape[0]
    tiles_per_seq = seq_len // tm
    it = pl.program_id(0) % tiles_per_seq
    g = g_ref[...]
    h = _rms(x, g)
    keep_prev = (it > 0).astype(F32)
    keep_next = (it < tiles_per_seq - 1).astype(F32)
    hp_ref[0:POOL_HALO, :] = _rms(xp_ref[...], g) * keep_prev
    hp_ref[POOL_HALO:POOL_HALO + tm, :] = h
    hp_ref[POOL_HALO + tm:, :] = _rms(xn_ref[...], g) * keep_next
    t = it * tm + lax.broadcasted_iota(jnp.int32, (tm, 1), 0)
    outs = []
    for gi, w in enumerate(POOL_WINDOWS):
        cols = slice(gi * POOL_GROUP_DIM, (gi + 1) * POOL_GROUP_DIM)
        s = hp_ref[POOL_HALO - w // 2:POOL_HALO - w // 2 + tm, cols]
        for k in range(1 - w // 2, w // 2):
            s = s + hp_ref[POOL_HALO + k:POOL_HALO + k + tm, cols]
        lo = jnp.clip(t - w // 2, 0, seq_len)
        hi = jnp.clip(t + w // 2, 0, seq_len)
        cnt = (hi - lo).astype(F32)
        pooled = s / cnt - h[:, cols]
        outs.append(_dot(pooled.astype(BF16), pw_ref[gi]))
    return x + jnp.concatenate(outs, axis=-1) * ps_ref[...]


def _kv_kernel(m_ref, g_ref, wkv_ref, kt_ref, v_ref):
    hm = _rms(m_ref[...], g_ref[...]).astype(BF16)
    kv = _dot(hm, wkv_ref[...])
    kt_ref[...] = kv[:, :D_MODEL].T.astype(BF16)
    v_ref[...] = kv[:, D_MODEL:].astype(BF16)


def _kv(mem, g, wkv):
    return pl.pallas_call(
        _kv_kernel,
        grid=(N_BATCH,),
        in_specs=[pl.BlockSpec((None, N_MEM, D_MODEL), lambda b: (b, 0, 0)),
                  _resident((1, D_MODEL)), _resident(wkv.shape)],
        out_specs=[pl.BlockSpec((None, D_MODEL, N_MEM), lambda b: (b, 0, 0)),
                   pl.BlockSpec((None, N_MEM, D_MODEL), lambda b: (b, 0, 0))],
        out_shape=[jax.ShapeDtypeStruct((N_BATCH, D_MODEL, N_MEM), BF16),
                   jax.ShapeDtypeStruct((N_BATCH, N_MEM, D_MODEL), BF16)],
        compiler_params=_cparams(("arbitrary",)),
        name="mem_kv",
    )(mem, g, wkv)


def _cross_attn(x, g_ref, wq_ref, kt_ref, v_ref, wo_ref):
    h = _rms(x, g_ref[...]).astype(BF16)
    q = (_dot(h, wq_ref[...]) * (CROSS_HEAD_DIM ** -0.5)).astype(BF16)
    heads = []
    for hd in range(CROSS_HEADS):
        cols = slice(hd * CROSS_HEAD_DIM, (hd + 1) * CROSS_HEAD_DIM)
        s = _dot(q[:, cols], kt_ref[cols, :])
        e = jnp.exp(s - jnp.max(s, axis=-1, keepdims=True))
        p = e / jnp.sum(e, axis=-1, keepdims=True)
        heads.append(_dot(p.astype(BF16), v_ref[:, cols]).astype(BF16))
    o = jnp.concatenate(heads, axis=-1)
    return x + _dot(o, wo_ref[...])


def _mix_cross_kernel(x_ref, yf_ref, ybk_ref, ua_ref, sg_ref, d_ref, wglu_ref, woa_ref, wob_ref,
                      g_ref, wq_ref, kt_ref, v_ref, wo_ref, o_ref):
    x = _mix_out(x_ref[...], yf_ref, ybk_ref, ua_ref, sg_ref, d_ref, wglu_ref, woa_ref, wob_ref)
    o_ref[...] = _cross_attn(x, g_ref, wq_ref, kt_ref, v_ref, wo_ref)


def _pool_cross_kernel(x_ref, xp_ref, xn_ref, mg_ref, pw_ref, ps_ref,
                       g_ref, wq_ref, kt_ref, v_ref, wo_ref, o_ref, hp_ref, *, seq_len):
    x = _pool(x_ref[...], xp_ref, xn_ref, mg_ref, pw_ref, ps_ref, hp_ref, seq_len=seq_len)
    o_ref[...] = _cross_attn(x, g_ref, wq_ref, kt_ref, v_ref, wo_ref)


def _cross_specs(seq_len, tm, wq, wo):
    tiles_per_seq = seq_len // tm
    return [_resident((1, D_MODEL)), _resident(wq.shape),
            pl.BlockSpec((None, D_MODEL, N_MEM), lambda i: (i // tiles_per_seq, 0, 0)),
            pl.BlockSpec((None, N_MEM, D_MODEL), lambda i: (i // tiles_per_seq, 0, 0)),
            _resident(wo.shape)]


def _mix_cross(x, seq_len, yf, ybk, ua, sg, d, wglu, woa, wob, g, wq, kt, v, wo):
    tm = TOK_ROWS
    n = x.shape[0]
    return pl.pallas_call(
        _mix_cross_kernel,
        grid=(n // tm,),
        in_specs=[_row_spec(tm, D_MODEL), _row_spec(tm, S5_WIDTH), _row_spec(tm, S5_WIDTH),
                  _row_spec(tm, S5_WIDTH), _row_spec(tm, SGU_WIDTH),
                  _resident((1, S5_WIDTH)), _resident(wglu.shape), _resident(woa.shape),
                  _resident(wob.shape)] + _cross_specs(seq_len, tm, wq, wo),
        out_specs=_row_spec(tm, D_MODEL),
        out_shape=jax.ShapeDtypeStruct((n, D_MODEL), F32),
        compiler_params=_cparams(("arbitrary",)),
        name="mix_cross",
    )(x, yf, ybk, ua, sg, d, wglu, woa, wob, g, wq, kt, v, wo)


def _pool_cross(x, seq_len, mg, pw, ps, g, wq, kt, v, wo):
    tm = TOK_ROWS
    n = x.shape[0]
    per = tm // POOL_HALO
    last = n // POOL_HALO - 1
    prev = pl.BlockSpec((POOL_HALO, D_MODEL), lambda i: (jnp.maximum(i * per - 1, 0), 0))
    nxt = pl.BlockSpec((POOL_HALO, D_MODEL), lambda i: (jnp.minimum((i + 1) * per, last), 0))
    return pl.pallas_call(
        functools.partial(_pool_cross_kernel, seq_len=seq_len),
        grid=(n // tm,),
        in_specs=[_row_spec(tm, D_MODEL), prev, nxt, _resident((1, D_MODEL)), _resident(pw.shape),
                  _resident((1, D_MODEL))] + _cross_specs(seq_len, tm, wq, wo),
        out_specs=_row_spec(tm, D_MODEL),
        out_shape=jax.ShapeDtypeStruct((n, D_MODEL), F32),
        scratch_shapes=[pltpu.VMEM((tm + 2 * POOL_HALO, D_MODEL), F32)],
        compiler_params=_cparams(("arbitrary",)),
        name="pool_cross",
    )(x, x, x, mg, pw, ps, g, wq, kt, v, wo)


def _trunk(x, mem, p):
    n_batch, seq_len, _ = x.shape
    assert n_batch == N_BATCH and seq_len % max(FFN_ROWS, TOK_ROWS) == 0 and seq_len % S5_CHUNK == 0
    x = x.reshape(n_batch * seq_len, D_MODEL)
    row = lambda a: a.reshape(1, -1)
    fg = row(p['final_norm'])
    for i in range(2):
        x = _ffn(x, row(p['ffn1_norm'][i]), p['ffn1_w_gate'][i], p['ffn1_w_up'][i],
                 p['ffn1_w_down'][i], fg, name=f"ffn1_l{i}")
        mg = row(p['mix_norm'][i])
        kt, v = _kv(mem, row(p['mem_norm'][i]), p['cross_w_kv'][i])
        cross = (row(p['cross_norm'][i]), p['cross_w_q'][i], kt, v, p['cross_w_o'][i])
        if i == 0:
            ua, sg = _inproj_sgu(x, mg, p['ab_w_in'], row(p['sgu_norm_g']), row(p['sgu_norm_b']),
                                 p['sgu_w_s'], p['sgu_b_s'])
            yf, ybk = _s5_scan(ua, seq_len, p['s5_wb'], p['s5_wc'], p['s5_a_re'], p['s5_a_im'])
            x = _mix_cross(x, seq_len, yf, ybk, ua, sg, row(p['s5_d']), p['s5_w_glu'],
                           p['ab_w_out_a'], p['ab_w_out_b'], *cross)
        else:
            x = _pool_cross(x, seq_len, mg, p['pool_w'], row(p['pool_scale']), *cross)
        x = _ffn(x, row(p['ffn2_norm'][i]), p['ffn2_w_gate'][i], p['ffn2_w_up'][i],
                 p['ffn2_w_down'][i], fg, final_norm=(i == 1), name=f"ffn2_l{i}")
    return x.reshape(n_batch, seq_len, D_MODEL)


def kernel(x_prompt, x_sample, mem_prompt, mem_sample, ffn1_norm, ffn1_w_gate, ffn1_w_up, ffn1_w_down, mix_norm, ab_w_in, s5_lambda_re, s5_lambda_im, s5_log_dt, s5_b_re, s5_b_im, s5_c_re, s5_c_im, s5_d, s5_w_glu, sgu_norm_g, sgu_norm_b, sgu_w_s, sgu_b_s, ab_w_out, pool_w, pool_scale, cross_norm, mem_norm, cross_w_q, cross_w_kv, cross_w_o, ffn2_norm, ffn2_w_gate, ffn2_w_up, ffn2_w_down, final_norm):
    bf = lambda a: a.astype(BF16)
    a_re, a_im, bb_re, bb_im = _s5_discretise(s5_lambda_re[0], s5_lambda_im[0], s5_log_dt[0],
                                              s5_b_re[0], s5_b_im[0])
    wb, wc = _s5_block_weights(bb_re, bb_im, s5_c_re[0], s5_c_im[0])
    p = dict(
        ffn1_norm=ffn1_norm, ffn1_w_gate=bf(ffn1_w_gate), ffn1_w_up=bf(ffn1_w_up),
        ffn1_w_down=bf(ffn1_w_down), mix_norm=mix_norm, ab_w_in=bf(ab_w_in[0]),
        s5_wb=wb, s5_wc=wc, s5_a_re=a_re, s5_a_im=a_im, s5_d=s5_d[0], s5_w_glu=bf(s5_w_glu[0]),
        sgu_norm_g=sgu_norm_g[0], sgu_norm_b=sgu_norm_b[0], sgu_w_s=bf(sgu_w_s[0]),
        sgu_b_s=jnp.broadcast_to(sgu_b_s[0][:, :, None], (SGU_HEADS, SGU_CHUNK, SGU_HEAD_DIM)),
        ab_w_out_a=bf(ab_w_out[0, :S5_WIDTH]), ab_w_out_b=bf(ab_w_out[0, S5_WIDTH:]),
        pool_w=bf(pool_w[0]), pool_scale=pool_scale[0],
        cross_norm=cross_norm, mem_norm=mem_norm, cross_w_q=bf(cross_w_q), cross_w_kv=bf(cross_w_kv),
        cross_w_o=bf(cross_w_o), ffn2_norm=ffn2_norm, ffn2_w_gate=bf(ffn2_w_gate),
        ffn2_w_up=bf(ffn2_w_up), ffn2_w_down=bf(ffn2_w_down), final_norm=final_norm)
    return (_trunk(x_prompt, mem_prompt, p), _trunk(x_sample, mem_sample, p))
```

```python
import functools
import math

import jax
import jax.numpy as jnp
from jax import lax
from jax.experimental import pallas as pl
from jax.experimental.pallas import tpu as pltpu

F32 = jnp.float32
BF16 = jnp.bfloat16

D_MODEL = 1024
D_FF = 4 * D_MODEL
N_BATCH = 8
N_MEM = 256
CROSS_HEADS = 4
CROSS_HEAD_DIM = D_MODEL // CROSS_HEADS
S5_WIDTH = D_MODEL // 2
S5_GROUP = 16
S5_GROUPS = S5_WIDTH // S5_GROUP
S5_STATE = 64
S5_NSTATE = S5_GROUPS * S5_STATE
SGU_WIDTH = D_MODEL // 2
SGU_HEADS = 4
SGU_HEAD_DIM = SGU_WIDTH // SGU_HEADS
SGU_CHUNK = 128
POOL_WINDOWS = (2, 4, 8, 16)
POOL_GROUP_DIM = D_MODEL // len(POOL_WINDOWS)
POOL_HALO = 16
EPS = 1e-6

V7X_LANES = 128
V7X_SUBLANES = 8
V7X_VMEM_BYTES = 64 * 1024 * 1024
VMEM_LIMIT_BYTES = 56 * 1024 * 1024

FFN_ROWS = 512
FFN_FCHUNK = 512
TOK_ROWS = 512
S5_CHUNK = 64
S5_BLOCK = 4
S5_PERM_STEPS = 32
S5_COLS = 512
S5_BTILE = 256
S5_CTILE = 512
S5_CH_BLOCKS = S5_WIDTH // V7X_LANES


def _cparams(sem):
    return pltpu.CompilerParams(dimension_semantics=sem, vmem_limit_bytes=VMEM_LIMIT_BYTES)


def _resident(shape):
    nd = len(shape)
    return pl.BlockSpec(shape, lambda *_: (0,) * nd, pipeline_mode=pl.Buffered(1))


def _row_spec(tm, width):
    return pl.BlockSpec((tm, width), lambda i: (i, 0))


def _rms(x, g):
    ms = jnp.mean(x * x, axis=-1, keepdims=True)
    return x * lax.rsqrt(ms + EPS) * g


def _gelu(x):
    c = math.sqrt(2.0 / math.pi)
    return x * (0.5 * (1.0 + jnp.tanh(c * (x + 0.044715 * (x * x * x)))))


def _sigmoid(x):
    return 1.0 / (1.0 + jnp.exp(-x))


def _dot(a, b):
    return jnp.dot(a, b, preferred_element_type=F32)


def _ffn_kernel(x_ref, g_ref, wg_ref, wu_ref, wd_ref, fg_ref, o_ref, *, final_norm):
    x = x_ref[...]
    h = _rms(x, g_ref[...]).astype(BF16)
    acc = jnp.zeros(x.shape, F32)
    for c in range(D_FF // FFN_FCHUNK):
        cols = slice(c * FFN_FCHUNK, (c + 1) * FFN_FCHUNK)
        gate = _dot(h, wg_ref[:, cols])
        up = _dot(h, wu_ref[:, cols])
        a = (gate * _sigmoid(gate) * up).astype(BF16)
        acc = acc + _dot(a, wd_ref[cols, :])
    y = x + 0.5 * acc
    if final_norm:
        y = _rms(y, fg_ref[...])
    o_ref[...] = y


def _ffn(x, g, wg, wu, wd, fg, *, final_norm=False, name):
    tm = FFN_ROWS
    n = x.shape[0]
    return pl.pallas_call(
        functools.partial(_ffn_kernel, final_norm=final_norm),
        grid=(n // tm,),
        in_specs=[_row_spec(tm, D_MODEL), _resident((1, D_MODEL)), _resident((D_MODEL, D_FF)),
                  _resident((D_MODEL, D_FF)), _resident((D_FF, D_MODEL)), _resident((1, D_MODEL))],
        out_specs=_row_spec(tm, D_MODEL),
        out_shape=jax.ShapeDtypeStruct((n, D_MODEL), F32),
        compiler_params=_cparams(("arbitrary",)),
        name=name,
    )(x, g, wg, wu, wd, fg)


def _inproj_sgu_kernel(x_ref, g_ref, win_ref, lng_ref, lnb_ref, ws_ref, bs_ref, ua_ref, yb_ref):
    h = _rms(x_ref[...], g_ref[...]).astype(BF16)
    z = _dot(h, win_ref[...])
    ua_ref[...] = z[:, :S5_WIDTH]
    u = _gelu(z[:, S5_WIDTH:S5_WIDTH + SGU_WIDTH])
    v = _gelu(z[:, S5_WIDTH + SGU_WIDTH:])
    mu = jnp.mean(v, axis=-1, keepdims=True)
    vc = v - mu
    var = jnp.mean(vc * vc, axis=-1, keepdims=True)
    v = (vc * lax.rsqrt(var + EPS) * lng_ref[...] + lnb_ref[...]).astype(BF16)
    for n in range(x_ref.shape[0] // SGU_CHUNK):
        rows = slice(n * SGU_CHUNK, (n + 1) * SGU_CHUNK)
        for hd in range(SGU_HEADS):
            cols = slice(hd * SGU_HEAD_DIM, (hd + 1) * SGU_HEAD_DIM)
            vs = _dot(ws_ref[hd], v[rows, cols]) + bs_ref[hd]
            yb_ref[rows, cols] = (u[rows, cols] * vs).astype(BF16)


def _inproj_sgu(x, g, win, lng, lnb, ws, bs):
    tm = TOK_ROWS
    n = x.shape[0]
    return pl.pallas_call(
        _inproj_sgu_kernel,
        grid=(n // tm,),
        in_specs=[_row_spec(tm, D_MODEL), _resident((1, D_MODEL)), _resident(win.shape),
                  _resident((1, SGU_WIDTH)), _resident((1, SGU_WIDTH)),
                  _resident(ws.shape), _resident(bs.shape)],
        out_specs=[_row_spec(tm, S5_WIDTH), _row_spec(tm, SGU_WIDTH)],
        out_shape=[jax.ShapeDtypeStruct((n, S5_WIDTH), F32),
                   jax.ShapeDtypeStruct((n, SGU_WIDTH), BF16)],
        compiler_params=_cparams(("arbitrary",)),
        name="inproj_sgu",
    )(x, g, win, lng, lnb, ws, bs)


def _s5_prep_kernel(lre_ref, lim_ref, ldt_ref, bre_ref, bim_ref, cre_ref, cim_ref,
                    are_ref, aim_ref, bpre_ref, bpim_ref, cpre_ref, cpim_ref, kk_ref):
    lam_re = lre_ref[...]
    lam_im = lim_ref[...]
    dt = jnp.exp(ldt_ref[...])
    mag = jnp.exp(lam_re * dt)
    ang = lam_im * dt
    a_re = mag * jnp.cos(ang)
    a_im = mag * jnp.sin(ang)
    nr = a_re - 1.0
    ni = a_im
    den = lam_re * lam_re + lam_im * lam_im
    q_re = ((nr * lam_re + ni * lam_im) / den)[:, None, :]
    q_im = ((ni * lam_re - nr * lam_im) / den)[:, None, :]
    b_re = bre_ref[...]
    b_im = bim_ref[...]
    bb_re = q_re * b_re - q_im * b_im
    bb_im = q_re * b_im + q_im * b_re
    c_re = cre_ref[...]
    c_im = cim_ref[...]

    p_re = jnp.ones_like(a_re)
    p_im = jnp.zeros_like(a_im)
    for k in range(S5_BLOCK + 1):
        pr = p_re[:, None, :]
        pi = p_im[:, None, :]
        if k < S5_BLOCK:
            bk_re = pr * bb_re - pi * bb_im
            bk_im = pr * bb_im + pi * bb_re
            bpre_ref[k] = bk_re
            bpim_ref[k] = bk_im
            for ch in range(S5_GROUP):
                term = c_re * bk_re[:, ch:ch + 1, :] - c_im * bk_im[:, ch:ch + 1, :]
                kk_ref[k, ch] = jnp.sum(term, axis=-1)
        if k >= 1:
            cpre_ref[k - 1] = pr * c_re - pi * c_im
            cpim_ref[k - 1] = pr * c_im + pi * c_re
        if k == S5_BLOCK:
            are_ref[...] = p_re
            aim_ref[...] = p_im
        p_re, p_im = p_re * a_re - p_im * a_im, p_re * a_im + p_im * a_re


def _s5_prepare(lam_re, lam_im, log_dt, b_re, b_im, c_re, c_im):
    g2 = 2 * S5_GROUPS
    r = S5_BLOCK
    flat = lambda a: a.reshape(g2, S5_STATE)
    gcp = lambda a: a.reshape(g2, S5_GROUP, S5_STATE)
    big = jax.ShapeDtypeStruct((r, g2, S5_GROUP, S5_STATE), F32)
    return pl.pallas_call(
        _s5_prep_kernel,
        out_shape=[jax.ShapeDtypeStruct((g2, S5_STATE), F32)] * 2 + [big] * 4
        + [jax.ShapeDtypeStruct((r, S5_GROUP, g2, S5_GROUP), F32)],
        name="s5_prepare",
    )(flat(lam_re), flat(lam_im), log_dt.reshape(g2, 1),
      gcp(jnp.swapaxes(b_re, -1, -2)), gcp(jnp.swapaxes(b_im, -1, -2)), gcp(c_re), gcp(c_im))


def _s5_block_weights(bp_re, bp_im, cp_re, cp_im, kk):
    r = S5_BLOCK
    g = S5_GROUPS
    eye = jnp.eye(g, dtype=F32)
    diag_b = lambda w: jnp.einsum('gcp,gh->gchp', w, eye).reshape(S5_WIDTH, S5_NSTATE)
    diag_c = lambda w: jnp.einsum('gcp,gh->gphc', w, eye).reshape(S5_NSTATE, S5_WIDTH)
    diag_k = lambda w: jnp.einsum('igc,gh->gihc', w, eye).reshape(S5_WIDTH, S5_WIDTH)
    ch_per_btile = S5_BTILE // S5_STATE * S5_GROUP
    lanes = V7X_LANES
    wb, wq, wt = [], [], []
    for d in range(2):
        grp = slice(d * g, (d + 1) * g)
        k_in = [r - 1 - s if d == 0 else s for s in range(r)]
        k_out = [s + 1 if d == 0 else r - s for s in range(r)]
        wb_d, wq_d = [], []
        for part, (bp, cp, sign) in enumerate(((bp_re, cp_re, 1.0), (bp_im, cp_im, -1.0))):
            bs = jnp.stack([diag_b(bp[k_in[s], grp]) for s in range(r)])
            tiles = []
            for j in range(S5_NSTATE // S5_BTILE):
                k0 = (j * ch_per_btile) // lanes * lanes
                tiles.append(bs[:, k0:k0 + lanes, j * S5_BTILE:(j + 1) * S5_BTILE].reshape(r * lanes, S5_BTILE))
            wb_d.append(jnp.stack(tiles))
            cs = jnp.stack([diag_c(sign * cp[k_out[s] - 1, grp]) for s in range(r)])
            tiles = []
            for m in range(S5_CH_BLOCKS):
                t = cs[:, m * S5_CTILE:(m + 1) * S5_CTILE, m * lanes:(m + 1) * lanes]
                tiles.append(jnp.swapaxes(t, 0, 1).reshape(S5_CTILE, r * lanes))
            wq_d.append(jnp.stack(tiles))
        wb.append(jnp.stack(wb_d))
        wq.append(jnp.stack(wq_d))
        kb = [diag_k(kk[k, :, grp, :]) for k in range(r)]
        tiles = []
        for m in range(S5_CH_BLOCKS):
            blk = slice(m * lanes, (m + 1) * lanes)
            zero = jnp.zeros((lanes, lanes), F32)
            rows = []
            for s_in in range(r):
                lag = [(s_out - s_in) if d == 0 else (s_in - s_out) for s_out in range(r)]
                rows.append(jnp.concatenate([kb[k][blk, blk] if k >= 0 else zero for k in lag], axis=1))
            tiles.append(jnp.concatenate(rows, axis=0))
        wt.append(jnp.stack(tiles))
    return jnp.stack(wb).astype(BF16), jnp.stack(wq).astype(BF16), jnp.stack(wt).astype(BF16)


def _s5_scan_kernel(uf_ref, ub_ref, perm_ref, permt_ref, wb_ref, wq_ref, wt_ref, are_ref, aim_ref,
                    yf_ref, yb_ref, utb_ref, up_ref, sre_ref, sim_ref, ytb_ref, st_ref):
    lc, r, nb, lanes = S5_CHUNK, S5_BLOCK, N_BATCH, V7X_LANES
    nblk = lc // r
    brows = nblk * nb
    prow = S5_PERM_STEPS * nb

    @pl.when(pl.program_id(0) == 0)
    def _():
        st_ref[...] = jnp.zeros(st_ref.shape, F32)

    for d, (u_ref, y_ref) in enumerate(((uf_ref, yf_ref), (ub_ref, yb_ref))):
        for h in range(lc // S5_PERM_STEPS):
            blk = u_ref[:, h * S5_PERM_STEPS:(h + 1) * S5_PERM_STEPS, :].reshape(prow, S5_WIDTH)
            utb_ref[h * prow:(h + 1) * prow, :] = _dot(perm_ref[...], blk.astype(BF16))
        for j in range(nblk):
            for s in range(r):
                slab = utb_ref[(j * r + s) * nb:(j * r + s + 1) * nb, :]
                for kb in range(S5_CH_BLOCKS):
                    up_ref[j * nb:(j + 1) * nb, (kb * r + s) * lanes:(kb * r + s + 1) * lanes] = (
                        slab[:, kb * lanes:(kb + 1) * lanes])

        off = nb if d == 0 else 0
        carry = 0 if d == 0 else brows
        ch_per_btile = S5_BTILE // S5_STATE * S5_GROUP
        for j in range(S5_NSTATE // S5_BTILE):
            kb = (j * ch_per_btile) // lanes
            cols = slice(j * S5_BTILE, (j + 1) * S5_BTILE)
            lhs = up_ref[:, kb * r * lanes:(kb + 1) * r * lanes].astype(BF16)
            sre_ref[off:off + brows, cols] = _dot(lhs, wb_ref[d, 0, j])
            sim_ref[off:off + brows, cols] = _dot(lhs, wb_ref[d, 1, j])

        for cb in range(S5_NSTATE // S5_COLS):
            cols = slice(cb * S5_COLS, (cb + 1) * S5_COLS)
            a_re = jnp.broadcast_to(are_ref[d:d + 1, cols], (nb, S5_COLS))
            a_im = jnp.broadcast_to(aim_ref[d:d + 1, cols], (nb, S5_COLS))
            h_re = st_ref[d, 0, :, cols]
            h_im = st_ref[d, 1, :, cols]
            sre_ref[carry:carry + nb, cols] = h_re
            sim_ref[carry:carry + nb, cols] = h_im
            for jj in range(nblk):
                j = jj if d == 0 else nblk - 1 - jj
                rows = slice(off + j * nb, off + (j + 1) * nb)
                h_re, h_im = (a_re * h_re - a_im * h_im + sre_ref[rows, cols],
                              a_re * h_im + a_im * h_re + sim_ref[rows, cols])
                sre_ref[rows, cols] = h_re
                sim_ref[rows, cols] = h_im
            st_ref[d, 0, :, cols] = h_re
            st_ref[d, 1, :, cols] = h_im

        enter = slice(0, brows) if d == 0 else slice(nb, brows + nb)
        for m in range(S5_CH_BLOCKS):
            srows = slice(m * S5_CTILE, (m + 1) * S5_CTILE)
            ym = (_dot(sre_ref[enter, srows].astype(BF16), wq_ref[d, 0, m])
                  + _dot(sim_ref[enter, srows].astype(BF16), wq_ref[d, 1, m])
                  + _dot(up_ref[:, m * r * lanes:(m + 1) * r * lanes].astype(BF16), wt_ref[d, m]))
            for j in range(nblk):
                for s in range(r):
                    ytb_ref[(j * r + s) * nb:(j * r + s + 1) * nb, m * lanes:(m + 1) * lanes] = (
                        ym[j * nb:(j + 1) * nb, s * lanes:(s + 1) * lanes])

        for h in range(lc // S5_PERM_STEPS):
            y = ytb_ref[h * prow:(h + 1) * prow, :]
            y_hi = y.astype(BF16)
            y_lo = (y - y_hi.astype(F32)).astype(BF16)
            y = _dot(permt_ref[...], y_hi) + _dot(permt_ref[...], y_lo)
            y_ref[:, h * S5_PERM_STEPS:(h + 1) * S5_PERM_STEPS, :] = y.reshape(nb, S5_PERM_STEPS, S5_WIDTH)


def _time_major_permutation():
    n = S5_PERM_STEPS * N_BATCH
    r = jnp.arange(n)
    src = (r % N_BATCH) * S5_PERM_STEPS + r // N_BATCH
    return jax.nn.one_hot(src, n, dtype=BF16)


def _s5_scan(ua, seq_len, wb, wq, wt, a_re, a_im):
    nc = seq_len // S5_CHUNK
    rows = S5_CHUNK * N_BATCH
    brows = rows // S5_BLOCK
    ua = ua.reshape(N_BATCH, seq_len, S5_WIDTH)
    perm = _time_major_permutation()
    fwd = pl.BlockSpec((N_BATCH, S5_CHUNK, S5_WIDTH), lambda c: (0, c, 0))
    bwd = pl.BlockSpec((N_BATCH, S5_CHUNK, S5_WIDTH), lambda c: (0, nc - 1 - c, 0))
    yf, yb = pl.pallas_call(
        _s5_scan_kernel,
        grid=(nc,),
        in_specs=[fwd, bwd, _resident(perm.shape), _resident(perm.shape), _resident(wb.shape),
                  _resident(wq.shape), _resident(wt.shape), _resident(a_re.shape), _resident(a_im.shape)],
        out_specs=[fwd, bwd],
        out_shape=[jax.ShapeDtypeStruct((N_BATCH, seq_len, S5_WIDTH), F32)] * 2,
        scratch_shapes=[pltpu.VMEM((rows, S5_WIDTH), F32),
                        pltpu.VMEM((brows, S5_BLOCK * S5_WIDTH), F32),
                        pltpu.VMEM((brows + N_BATCH, S5_NSTATE), F32),
                        pltpu.VMEM((brows + N_BATCH, S5_NSTATE), F32),
                        pltpu.VMEM((rows, S5_WIDTH), F32),
                        pltpu.VMEM((2, 2, N_BATCH, S5_NSTATE), F32)],
        compiler_params=_cparams(("arbitrary",)),
        name="s5_scan",
    )(ua, ua, perm, perm.T, wb, wq, wt, a_re, a_im)
    n = N_BATCH * seq_len
    return yf.reshape(n, S5_WIDTH), yb.reshape(n, S5_WIDTH)


def _mix_out(x, yf_ref, ybk_ref, ua_ref, sg_ref, d_ref, wglu_ref, woa_ref, wob_ref):
    y = yf_ref[...] + ybk_ref[...] + d_ref[...] * ua_ref[...]
    g = _gelu(y)
    ya = g * _sigmoid(_dot(g.astype(BF16), wglu_ref[...]))
    return x + _dot(ya.astype(BF16), woa_ref[...]) + _dot(sg_ref[...], wob_ref[...])


def _pool(x, xp_ref, xn_ref, g_ref, pw_ref, ps_ref, hp_ref, sa_ref, sb_ref, *, seq_len):
    tm = x.shape[0]
    halo = POOL_HALO
    tiles_per_seq = seq_len // tm
    it = pl.program_id(0) % tiles_per_seq
    g = g_ref[...]
    h = _rms(x, g)
    keep_prev = (it > 0).astype(F32)
    keep_next = (it < tiles_per_seq - 1).astype(F32)
    hp_ref[0:halo, :] = _rms(xp_ref[...], g) * keep_prev
    hp_ref[halo:halo + tm, :] = h
    hp_ref[halo + tm:, :] = _rms(xn_ref[...], g) * keep_next
    lo, n = halo // 2, tm + halo
    tail = slice(lo + n, tm + 2 * halo)
    sa_ref[tail, :] = jnp.zeros((halo // 2, D_MODEL), F32)
    sb_ref[tail, :] = jnp.zeros((halo // 2, D_MODEL), F32)

    def doubled(src_ref, dst_ref, step, cols):
        dst_ref[lo:lo + n, cols] = src_ref[lo:lo + n, cols] + src_ref[lo + step:lo + step + n, cols]

    def window(ref, first, cols):
        return ref[first:first + tm, cols] + ref[halo:halo + tm, cols]

    t = it * tm + lax.broadcasted_iota(jnp.int32, (tm, 1), 0)
    outs = []
    for gi, w in enumerate(POOL_WINDOWS):
        cols = slice(gi * POOL_GROUP_DIM, (gi + 1) * POOL_GROUP_DIM)
        if w == 2:
            s = window(hp_ref, halo - 1, cols)
        elif w == 4:
            doubled(hp_ref, sa_ref, 1, cols)
            s = window(sa_ref, halo - 2, cols)
        elif w == 8:
            doubled(hp_ref, sa_ref, 1, cols)
            doubled(sa_ref, sb_ref, 2, cols)
            s = window(sb_ref, halo - 4, cols)
        else:
            assert w == 16
            doubled(hp_ref, sa_ref, 1, cols)
            doubled(sa_ref, sb_ref, 2, cols)
            doubled(sb_ref, sa_ref, 4, cols)
            s = window(sa_ref, halo - 8, cols)
        lo_t = jnp.clip(t - w // 2, 0, seq_len)
        hi_t = jnp.clip(t + w // 2, 0, seq_len)
        cnt = (hi_t - lo_t).astype(F32)
        pooled = s / cnt - h[:, cols]
        outs.append(_dot(pooled.astype(BF16), pw_ref[gi]))
    return x + jnp.concatenate(outs, axis=-1) * ps_ref[...]


def _kv_kernel(m_ref, g_ref, wkv_ref, kt_ref, v_ref):
    hm = _rms(m_ref[...], g_ref[...]).astype(BF16)
    kv = _dot(hm, wkv_ref[...])
    kt_ref[...] = kv[:, :D_MODEL].T.astype(BF16)
    v_ref[...] = kv[:, D_MODEL:].astype(BF16)


def _kv(mem, g, wkv):
    return pl.pallas_call(
        _kv_kernel,
        grid=(N_BATCH,),
        in_specs=[pl.BlockSpec((None, N_MEM, D_MODEL), lambda b: (b, 0, 0)),
                  _resident((1, D_MODEL)), _resident(wkv.shape)],
        out_specs=[pl.BlockSpec((None, D_MODEL, N_MEM), lambda b: (b, 0, 0)),
                   pl.BlockSpec((None, N_MEM, D_MODEL), lambda b: (b, 0, 0))],
        out_shape=[jax.ShapeDtypeStruct((N_BATCH, D_MODEL, N_MEM), BF16),
                   jax.ShapeDtypeStruct((N_BATCH, N_MEM, D_MODEL), BF16)],
        compiler_params=_cparams(("arbitrary",)),
        name="mem_kv",
    )(mem, g, wkv)


def _cross_attn(x, g_ref, wq_ref, kt_ref, v_ref, wo_ref):
    h = _rms(x, g_ref[...]).astype(BF16)
    q = (_dot(h, wq_ref[...]) * (CROSS_HEAD_DIM ** -0.5)).astype(BF16)
    heads = []
    for hd in range(CROSS_HEADS):
        cols = slice(hd * CROSS_HEAD_DIM, (hd + 1) * CROSS_HEAD_DIM)
        s = _dot(q[:, cols], kt_ref[cols, :])
        e = jnp.exp(s - jnp.max(s, axis=-1, keepdims=True))
        p = e * (1.0 / jnp.sum(e, axis=-1, keepdims=True))
        heads.append(_dot(p.astype(BF16), v_ref[:, cols]).astype(BF16))
    o = jnp.concatenate(heads, axis=-1)
    return x + _dot(o, wo_ref[...])


def _mix_cross_kernel(x_ref, yf_ref, ybk_ref, ua_ref, sg_ref, d_ref, wglu_ref, woa_ref, wob_ref,
                      g_ref, wq_ref, kt_ref, v_ref, wo_ref, o_ref):
    x = _mix_out(x_ref[...], yf_ref, ybk_ref, ua_ref, sg_ref, d_ref, wglu_ref, woa_ref, wob_ref)
    o_ref[...] = _cross_attn(x, g_ref, wq_ref, kt_ref, v_ref, wo_ref)


def _pool_cross_kernel(x_ref, xp_ref, xn_ref, mg_ref, pw_ref, ps_ref,
                       g_ref, wq_ref, kt_ref, v_ref, wo_ref, o_ref, hp_ref, sa_ref, sb_ref, *, seq_len):
    x = _pool(x_ref[...], xp_ref, xn_ref, mg_ref, pw_ref, ps_ref, hp_ref, sa_ref, sb_ref, seq_len=seq_len)
    o_ref[...] = _cross_attn(x, g_ref, wq_ref, kt_ref, v_ref, wo_ref)


def _cross_specs(seq_len, tm, wq, wo):
    tiles_per_seq = seq_len // tm
    return [_resident((1, D_MODEL)), _resident(wq.shape),
            pl.BlockSpec((None, D_MODEL, N_MEM), lambda i: (i // tiles_per_seq, 0, 0)),
            pl.BlockSpec((None, N_MEM, D_MODEL), lambda i: (i // tiles_per_seq, 0, 0)),
            _resident(wo.shape)]


def _mix_cross(x, seq_len, yf, ybk, ua, sg, d, wglu, woa, wob, g, wq, kt, v, wo):
    tm = TOK_ROWS
    n = x.shape[0]
    return pl.pallas_call(
        _mix_cross_kernel,
        grid=(n // tm,),
        in_specs=[_row_spec(tm, D_MODEL), _row_spec(tm, S5_WIDTH), _row_spec(tm, S5_WIDTH),
                  _row_spec(tm, S5_WIDTH), _row_spec(tm, SGU_WIDTH),
                  _resident((1, S5_WIDTH)), _resident(wglu.shape), _resident(woa.shape),
                  _resident(wob.shape)] + _cross_specs(seq_len, tm, wq, wo),
        out_specs=_row_spec(tm, D_MODEL),
        out_shape=jax.ShapeDtypeStruct((n, D_MODEL), F32),
        compiler_params=_cparams(("arbitrary",)),
        name="mix_cross",
    )(x, yf, ybk, ua, sg, d, wglu, woa, wob, g, wq, kt, v, wo)


def _pool_cross(x, seq_len, mg, pw, ps, g, wq, kt, v, wo):
    tm = TOK_ROWS
    n = x.shape[0]
    per = tm // POOL_HALO
    last = n // POOL_HALO - 1
    prev = pl.BlockSpec((POOL_HALO, D_MODEL), lambda i: (jnp.maximum(i * per - 1, 0), 0))
    nxt = pl.BlockSpec((POOL_HALO, D_MODEL), lambda i: (jnp.minimum((i + 1) * per, last), 0))
    padded = pltpu.VMEM((tm + 2 * POOL_HALO, D_MODEL), F32)
    return pl.pallas_call(
        functools.partial(_pool_cross_kernel, seq_len=seq_len),
        grid=(n // tm,),
        in_specs=[_row_spec(tm, D_MODEL), prev, nxt, _resident((1, D_MODEL)), _resident(pw.shape),
                  _resident((1, D_MODEL))] + _cross_specs(seq_len, tm, wq, wo),
        out_specs=_row_spec(tm, D_MODEL),
        out_shape=jax.ShapeDtypeStruct((n, D_MODEL), F32),
        scratch_shapes=[padded, padded, padded],
        compiler_params=_cparams(("arbitrary",)),
        name="pool_cross",
    )(x, x, x, mg, pw, ps, g, wq, kt, v, wo)


def _trunk(x, mem, p):
    n_batch, seq_len, _ = x.shape
    assert n_batch == N_BATCH and seq_len % max(FFN_ROWS, TOK_ROWS) == 0 and seq_len % S5_CHUNK == 0
    x = x.reshape(n_batch * seq_len, D_MODEL)
    row = lambda a: a.reshape(1, -1)
    fg = row(p['final_norm'])
    for i in range(2):
        x = _ffn(x, row(p['ffn1_norm'][i]), p['ffn1_w_gate'][i], p['ffn1_w_up'][i],
                 p['ffn1_w_down'][i], fg, name=f"ffn1_l{i}")
        mg = row(p['mix_norm'][i])
        kt, v = _kv(mem, row(p['mem_norm'][i]), p['cross_w_kv'][i])
        cross = (row(p['cross_norm'][i]), p['cross_w_q'][i], kt, v, p['cross_w_o'][i])
        if i == 0:
            ua, sg = _inproj_sgu(x, mg, p['ab_w_in'], row(p['sgu_norm_g']), row(p['sgu_norm_b']),
                                 p['sgu_w_s'], p['sgu_b_s'])
            yf, ybk = _s5_scan(ua, seq_len, p['s5_wb'], p['s5_wq'], p['s5_wt'], p['s5_a_re'], p['s5_a_im'])
            x = _mix_cross(x, seq_len, yf, ybk, ua, sg, row(p['s5_d']), p['s5_w_glu'],
                           p['ab_w_out_a'], p['ab_w_out_b'], *cross)
        else:
            x = _pool_cross(x, seq_len, mg, p['pool_w'], row(p['pool_scale']), *cross)
        x = _ffn(x, row(p['ffn2_norm'][i]), p['ffn2_w_gate'][i], p['ffn2_w_up'][i],
                 p['ffn2_w_down'][i], fg, final_norm=(i == 1), name=f"ffn2_l{i}")
    return x.reshape(n_batch, seq_len, D_MODEL)


def kernel(x_prompt, x_sample, mem_prompt, mem_sample, ffn1_norm, ffn1_w_gate, ffn1_w_up, ffn1_w_down, mix_norm, ab_w_in, s5_lambda_re, s5_lambda_im, s5_log_dt, s5_b_re, s5_b_im, s5_c_re, s5_c_im, s5_d, s5_w_glu, sgu_norm_g, sgu_norm_b, sgu_w_s, sgu_b_s, ab_w_out, pool_w, pool_scale, cross_norm, mem_norm, cross_w_q, cross_w_kv, cross_w_o, ffn2_norm, ffn2_w_gate, ffn2_w_up, ffn2_w_down, final_norm):
    bf = lambda a: a.astype(BF16)
    ar_re, ar_im, bp_re, bp_im, cp_re, cp_im, kk = _s5_prepare(
        s5_lambda_re[0], s5_lambda_im[0], s5_log_dt[0], s5_b_re[0], s5_b_im[0], s5_c_re[0], s5_c_im[0])
    wb, wq, wt = _s5_block_weights(bp_re, bp_im, cp_re, cp_im, kk)
    p = dict(
        ffn1_norm=ffn1_norm, ffn1_w_gate=bf(ffn1_w_gate), ffn1_w_up=bf(ffn1_w_up),
        ffn1_w_down=bf(ffn1_w_down), mix_norm=mix_norm, ab_w_in=bf(ab_w_in[0]),
        s5_wb=wb, s5_wq=wq, s5_wt=wt,
        s5_a_re=ar_re.reshape(2, S5_NSTATE), s5_a_im=ar_im.reshape(2, S5_NSTATE),
        s5_d=s5_d[0], s5_w_glu=bf(s5_w_glu[0]),
        sgu_norm_g=sgu_norm_g[0], sgu_norm_b=sgu_norm_b[0], sgu_w_s=bf(sgu_w_s[0]),
        sgu_b_s=jnp.broadcast_to(sgu_b_s[0][:, :, None], (SGU_HEADS, SGU_CHUNK, SGU_HEAD_DIM)),
        ab_w_out_a=bf(ab_w_out[0, :S5_WIDTH]), ab_w_out_b=bf(ab_w_out[0, S5_WIDTH:]),
        pool_w=bf(pool_w[0]), pool_scale=pool_scale[0],
        cross_norm=cross_norm, mem_norm=mem_norm, cross_w_q=bf(cross_w_q), cross_w_kv=bf(cross_w_kv),
        cross_w_o=bf(cross_w_o), ffn2_norm=ffn2_norm, ffn2_w_gate=bf(ffn2_w_gate),
        ffn2_w_up=bf(ffn2_w_up), ffn2_w_down=bf(ffn2_w_down), final_norm=final_norm)
    return (_trunk(x_prompt, mem_prompt, p), _trunk(x_sample, mem_sample, p))
```

```python
import functools
import math

import jax
import jax.numpy as jnp
from jax import lax
from jax.experimental import pallas as pl
from jax.experimental.pallas import tpu as pltpu

F32 = jnp.float32
BF16 = jnp.bfloat16

D_MODEL = 1024
D_FF = 4 * D_MODEL
N_BATCH = 8
N_MEM = 256
CROSS_HEADS = 4
CROSS_HEAD_DIM = D_MODEL // CROSS_HEADS
S5_WIDTH = D_MODEL // 2
S5_GROUP = 16
S5_GROUPS = S5_WIDTH // S5_GROUP
S5_STATE = 64
S5_NSTATE = S5_GROUPS * S5_STATE
SGU_WIDTH = D_MODEL // 2
SGU_HEADS = 4
SGU_HEAD_DIM = SGU_WIDTH // SGU_HEADS
SGU_CHUNK = 128
POOL_WINDOWS = (2, 4, 8, 16)
POOL_GROUP_DIM = D_MODEL // len(POOL_WINDOWS)
POOL_HALO = 16
EPS = 1e-6

V7X_LANES = 128
V7X_SUBLANES = 8
V7X_VMEM_BYTES = 64 * 1024 * 1024
VMEM_LIMIT_BYTES = 56 * 1024 * 1024

FFN_ROWS = 512
FFN_FCHUNK = 512
TOK_ROWS = 1024
S5_CHUNK = 64
S5_BLOCK = 4
S5_PERM_STEPS = 32
S5_COLS = 512
S5_BTILE = 256
S5_CTILE = 512
S5_CH_BLOCKS = S5_WIDTH // V7X_LANES


def _cparams(sem):
    return pltpu.CompilerParams(dimension_semantics=sem, vmem_limit_bytes=VMEM_LIMIT_BYTES)


def _resident(shape):
    nd = len(shape)
    return pl.BlockSpec(shape, lambda *_: (0,) * nd, pipeline_mode=pl.Buffered(1))


def _row_spec(tm, width):
    return pl.BlockSpec((tm, width), lambda i: (i, 0))


def _rms(x, g):
    ms = jnp.mean(x * x, axis=-1, keepdims=True)
    return x * lax.rsqrt(ms + EPS) * g


def _gelu(x):
    c = math.sqrt(2.0 / math.pi)
    return x * (0.5 * (1.0 + jnp.tanh(c * (x + 0.044715 * (x * x * x)))))


def _sigmoid(x):
    return 1.0 / (1.0 + jnp.exp(-x))


def _dot(a, b):
    return jnp.dot(a, b, preferred_element_type=F32)


def _ffn_kernel(x_ref, g_ref, wg_ref, wu_ref, wd_ref, fg_ref, o_ref, *, final_norm):
    x = x_ref[...]
    h = _rms(x, g_ref[...]).astype(BF16)
    acc = jnp.zeros(x.shape, F32)
    for c in range(D_FF // FFN_FCHUNK):
        cols = slice(c * FFN_FCHUNK, (c + 1) * FFN_FCHUNK)
        gate = _dot(h, wg_ref[:, cols])
        up = _dot(h, wu_ref[:, cols])
        a = (gate * _sigmoid(gate) * up).astype(BF16)
        acc = acc + _dot(a, wd_ref[cols, :])
    y = x + 0.5 * acc
    if final_norm:
        y = _rms(y, fg_ref[...])
    o_ref[...] = y


def _ffn(x, g, wg, wu, wd, fg, *, final_norm=False, name):
    tm = FFN_ROWS
    n = x.shape[0]
    return pl.pallas_call(
        functools.partial(_ffn_kernel, final_norm=final_norm),
        grid=(n // tm,),
        in_specs=[_row_spec(tm, D_MODEL), _resident((1, D_MODEL)), _resident((D_MODEL, D_FF)),
                  _resident((D_MODEL, D_FF)), _resident((D_FF, D_MODEL)), _resident((1, D_MODEL))],
        out_specs=_row_spec(tm, D_MODEL),
        out_shape=jax.ShapeDtypeStruct((n, D_MODEL), F32),
        compiler_params=_cparams(("arbitrary",)),
        name=name,
    )(x, g, wg, wu, wd, fg)


def _inproj_sgu_kernel(x_ref, g_ref, win_ref, lng_ref, lnb_ref, ws_ref, bs_ref, ua_ref, yb_ref):
    h = _rms(x_ref[...], g_ref[...]).astype(BF16)
    z = _dot(h, win_ref[...])
    ua_ref[...] = z[:, :S5_WIDTH]
    u = _gelu(z[:, S5_WIDTH:S5_WIDTH + SGU_WIDTH])
    v = _gelu(z[:, S5_WIDTH + SGU_WIDTH:])
    mu = jnp.mean(v, axis=-1, keepdims=True)
    vc = v - mu
    var = jnp.mean(vc * vc, axis=-1, keepdims=True)
    v = (vc * lax.rsqrt(var + EPS) * lng_ref[...] + lnb_ref[...]).astype(BF16)
    for n in range(x_ref.shape[0] // SGU_CHUNK):
        rows = slice(n * SGU_CHUNK, (n + 1) * SGU_CHUNK)
        for hd in range(SGU_HEADS):
            cols = slice(hd * SGU_HEAD_DIM, (hd + 1) * SGU_HEAD_DIM)
            vs = _dot(ws_ref[hd], v[rows, cols]) + bs_ref[hd]
            yb_ref[rows, cols] = (u[rows, cols] * vs).astype(BF16)


def _inproj_sgu(x, g, win, lng, lnb, ws, bs):
    tm = TOK_ROWS
    n = x.shape[0]
    return pl.pallas_call(
        _inproj_sgu_kernel,
        grid=(n // tm,),
        in_specs=[_row_spec(tm, D_MODEL), _resident((1, D_MODEL)), _resident(win.shape),
                  _resident((1, SGU_WIDTH)), _resident((1, SGU_WIDTH)),
                  _resident(ws.shape), _resident(bs.shape)],
        out_specs=[_row_spec(tm, S5_WIDTH), _row_spec(tm, SGU_WIDTH)],
        out_shape=[jax.ShapeDtypeStruct((n, S5_WIDTH), F32),
                   jax.ShapeDtypeStruct((n, SGU_WIDTH), BF16)],
        compiler_params=_cparams(("arbitrary",)),
        name="inproj_sgu",
    )(x, g, win, lng, lnb, ws, bs)


def _zoh_a(lam_re, lam_im, log_dt):
    dt = jnp.exp(log_dt)
    mag = jnp.exp(lam_re * dt)
    ang = lam_im * dt
    return mag * jnp.cos(ang), mag * jnp.sin(ang)


def _s5_prep_kernel(lre_ref, lim_ref, ldt_ref, bre_ref, bim_ref, cre_ref, cim_ref,
                    lre3_ref, lim3_ref, ldt3_ref, ctre_ref, ctim_ref,
                    are_ref, aim_ref, wb_ref, wq_ref, wt_ref):
    r, g, grp, lanes = S5_BLOCK, S5_GROUPS, S5_GROUP, V7X_LANES
    lam_re = lre_ref[...]
    lam_im = lim_ref[...]
    a_re, a_im = _zoh_a(lam_re, lam_im, ldt_ref[...])
    nr = a_re - 1.0
    ni = a_im
    den = lam_re * lam_re + lam_im * lam_im
    q_re = ((nr * lam_re + ni * lam_im) / den)[:, None, :]
    q_im = ((ni * lam_re - nr * lam_im) / den)[:, None, :]
    b_re = bre_ref[...]
    b_im = bim_ref[...]
    bb_re = q_re * b_re - q_im * b_im
    bb_im = q_re * b_im + q_im * b_re
    c_re = cre_ref[...]
    c_im = cim_ref[...]
    a3_re, a3_im = _zoh_a(lre3_ref[...], lim3_ref[...], ldt3_ref[...])
    ct_re = ctre_ref[...]
    ct_im = ctim_ref[...]

    bk, ck, kk = [], [], []
    p_re, p_im = jnp.ones_like(a_re), jnp.zeros_like(a_im)
    p3_re, p3_im = jnp.ones_like(a3_re), jnp.zeros_like(a3_im)
    contract_states = (((2,), (2,)), ((0,), (0,)))
    for k in range(r + 1):
        if k < r:
            pr, pi = p_re[:, None, :], p_im[:, None, :]
            bk_re = pr * bb_re - pi * bb_im
            bk_im = pr * bb_im + pi * bb_re
            bk.append((bk_re, bk_im))
            kk.append(lax.dot_general(bk_re, c_re, contract_states, precision=lax.Precision.HIGHEST,
                                      preferred_element_type=F32)
                      - lax.dot_general(bk_im, c_im, contract_states, precision=lax.Precision.HIGHEST,
                                        preferred_element_type=F32))
        if k >= 1:
            ck.append((p3_re * ct_re - p3_im * ct_im, -(p3_re * ct_im + p3_im * ct_re)))
        if k == r:
            are_ref[...] = p_re
            aim_ref[...] = p_im
        p_re, p_im = p_re * a_re - p_im * a_im, p_re * a_im + p_im * a_re
        p3_re, p3_im = p3_re * a3_re - p3_im * a3_im, p3_re * a3_im + p3_im * a3_re

    wb_ref[...] = jnp.zeros(wb_ref.shape, BF16)
    wq_ref[...] = jnp.zeros(wq_ref.shape, BF16)
    wt_ref[...] = jnp.zeros(wt_ref.shape, BF16)
    groups_per_btile = S5_BTILE // S5_STATE
    groups_per_lanes = lanes // grp
    for d in range(2):
        k_in = [r - 1 - s if d == 0 else s for s in range(r)]
        k_out = [s + 1 if d == 0 else r - s for s in range(r)]
        for j in range(S5_NSTATE // S5_BTILE):
            for gt in range(groups_per_btile):
                gi = j * groups_per_btile + gt
                gl = gi % groups_per_lanes
                for s in range(r):
                    rows = slice(s * lanes + gl * grp, s * lanes + (gl + 1) * grp)
                    cols = slice(gt * S5_STATE, (gt + 1) * S5_STATE)
                    for part in range(2):
                        wb_ref[d, part, j, rows, cols] = bk[k_in[s]][part][d * g + gi].astype(BF16)
        for m in range(S5_CH_BLOCKS):
            for gl in range(groups_per_lanes):
                gi = m * groups_per_lanes + gl
                srows = slice(gl * S5_STATE, (gl + 1) * S5_STATE)
                for s in range(r):
                    cols = slice(s * lanes + gl * grp, s * lanes + (gl + 1) * grp)
                    for part in range(2):
                        wq_ref[d, part, m, srows, cols] = ck[k_out[s] - 1][part][d * g + gi].astype(BF16)
                    for s_out in range(r):
                        lag = (s_out - s) if d == 0 else (s - s_out)
                        if lag >= 0:
                            ocols = slice(s_out * lanes + gl * grp, s_out * lanes + (gl + 1) * grp)
                            wt_ref[d, m, cols, ocols] = kk[lag][d * g + gi].astype(BF16)


def _s5_prepare(lam_re, lam_im, log_dt, b_re, b_im, c_re, c_im):
    g2 = 2 * S5_GROUPS
    r = S5_BLOCK
    rl = r * V7X_LANES
    flat = lambda a: a.reshape(g2, S5_STATE)
    gcp = lambda a: a.reshape(g2, S5_GROUP, S5_STATE)
    gpc = lambda a: jnp.swapaxes(gcp(a), -1, -2)
    on_sublanes = lambda a: jnp.broadcast_to(a.reshape(g2, -1, 1), (g2, S5_STATE, S5_GROUP))
    a_re, a_im, wb, wq, wt = pl.pallas_call(
        _s5_prep_kernel,
        out_shape=[jax.ShapeDtypeStruct((g2, S5_STATE), F32)] * 2
        + [jax.ShapeDtypeStruct((2, 2, S5_NSTATE // S5_BTILE, rl, S5_BTILE), BF16),
           jax.ShapeDtypeStruct((2, 2, S5_CH_BLOCKS, S5_CTILE, rl), BF16),
           jax.ShapeDtypeStruct((2, S5_CH_BLOCKS, rl, rl), BF16)],
        compiler_params=pltpu.CompilerParams(vmem_limit_bytes=VMEM_LIMIT_BYTES),
        name="s5_prepare",
    )(flat(lam_re), flat(lam_im), log_dt.reshape(g2, 1),
      gcp(jnp.swapaxes(b_re, -1, -2)), gcp(jnp.swapaxes(b_im, -1, -2)), gcp(c_re), gcp(c_im),
      on_sublanes(lam_re), on_sublanes(lam_im), on_sublanes(log_dt), gpc(c_re), gpc(c_im))
    return a_re.reshape(2, S5_NSTATE), a_im.reshape(2, S5_NSTATE), wb, wq, wt


def _s5_scan_kernel(uf_ref, ub_ref, perm_ref, permt_ref, wb_ref, wq_ref, wt_ref, are_ref, aim_ref,
                    yf_ref, yb_ref, utb_ref, up_ref, sre_ref, sim_ref, ytb_ref, st_ref):
    lc, r, nb, lanes = S5_CHUNK, S5_BLOCK, N_BATCH, V7X_LANES
    nblk = lc // r
    brows = nblk * nb
    prow = S5_PERM_STEPS * nb

    @pl.when(pl.program_id(0) == 0)
    def _():
        st_ref[...] = jnp.zeros(st_ref.shape, F32)

    for d, (u_ref, y_ref) in enumerate(((uf_ref, yf_ref), (ub_ref, yb_ref))):
        for h in range(lc // S5_PERM_STEPS):
            blk = u_ref[:, h * S5_PERM_STEPS:(h + 1) * S5_PERM_STEPS, :].reshape(prow, S5_WIDTH)
            utb_ref[h * prow:(h + 1) * prow, :] = _dot(perm_ref[...], blk.astype(BF16))
        for j in range(nblk):
            for s in range(r):
                slab = utb_ref[(j * r + s) * nb:(j * r + s + 1) * nb, :]
                for kb in range(S5_CH_BLOCKS):
                    up_ref[j * nb:(j + 1) * nb, (kb * r + s) * lanes:(kb * r + s + 1) * lanes] = (
                        slab[:, kb * lanes:(kb + 1) * lanes])

        off = nb if d == 0 else 0
        carry = 0 if d == 0 else brows
        ch_per_btile = S5_BTILE // S5_STATE * S5_GROUP
        for j in range(S5_NSTATE // S5_BTILE):
            kb = (j * ch_per_btile) // lanes
            cols = slice(j * S5_BTILE, (j + 1) * S5_BTILE)
            lhs = up_ref[:, kb * r * lanes:(kb + 1) * r * lanes].astype(BF16)
            sre_ref[off:off + brows, cols] = _dot(lhs, wb_ref[d, 0, j])
            sim_ref[off:off + brows, cols] = _dot(lhs, wb_ref[d, 1, j])

        for cb in range(S5_NSTATE // S5_COLS):
            cols = slice(cb * S5_COLS, (cb + 1) * S5_COLS)
            a_re = jnp.broadcast_to(are_ref[d:d + 1, cols], (nb, S5_COLS))
            a_im = jnp.broadcast_to(aim_ref[d:d + 1, cols], (nb, S5_COLS))
            h_re = st_ref[d, 0, :, cols]
            h_im = st_ref[d, 1, :, cols]
            sre_ref[carry:carry + nb, cols] = h_re
            sim_ref[carry:carry + nb, cols] = h_im
            for jj in range(nblk):
                j = jj if d == 0 else nblk - 1 - jj
                rows = slice(off + j * nb, off + (j + 1) * nb)
                h_re, h_im = (a_re * h_re - a_im * h_im + sre_ref[rows, cols],
                              a_re * h_im + a_im * h_re + sim_ref[rows, cols])
                sre_ref[rows, cols] = h_re
                sim_ref[rows, cols] = h_im
            st_ref[d, 0, :, cols] = h_re
            st_ref[d, 1, :, cols] = h_im

        enter = slice(0, brows) if d == 0 else slice(nb, brows + nb)
        for m in range(S5_CH_BLOCKS):
            srows = slice(m * S5_CTILE, (m + 1) * S5_CTILE)
            ym = (_dot(sre_ref[enter, srows].astype(BF16), wq_ref[d, 0, m])
                  + _dot(sim_ref[enter, srows].astype(BF16), wq_ref[d, 1, m])
                  + _dot(up_ref[:, m * r * lanes:(m + 1) * r * lanes].astype(BF16), wt_ref[d, m]))
            for j in range(nblk):
                for s in range(r):
                    ytb_ref[(j * r + s) * nb:(j * r + s + 1) * nb, m * lanes:(m + 1) * lanes] = (
                        ym[j * nb:(j + 1) * nb, s * lanes:(s + 1) * lanes])

        for h in range(lc // S5_PERM_STEPS):
            y = ytb_ref[h * prow:(h + 1) * prow, :]
            y_hi = y.astype(BF16)
            y_lo = (y - y_hi.astype(F32)).astype(BF16)
            y = _dot(permt_ref[...], y_hi) + _dot(permt_ref[...], y_lo)
            y_ref[:, h * S5_PERM_STEPS:(h + 1) * S5_PERM_STEPS, :] = y.reshape(nb, S5_PERM_STEPS, S5_WIDTH)


def _time_major_permutation():
    n = S5_PERM_STEPS * N_BATCH
    r = jnp.arange(n)
    src = (r % N_BATCH) * S5_PERM_STEPS + r // N_BATCH
    return jax.nn.one_hot(src, n, dtype=BF16)


def _s5_scan(ua, seq_len, wb, wq, wt, a_re, a_im):
    nc = seq_len // S5_CHUNK
    rows = S5_CHUNK * N_BATCH
    brows = rows // S5_BLOCK
    ua = ua.reshape(N_BATCH, seq_len, S5_WIDTH)
    perm = _time_major_permutation()
    fwd = pl.BlockSpec((N_BATCH, S5_CHUNK, S5_WIDTH), lambda c: (0, c, 0))
    bwd = pl.BlockSpec((N_BATCH, S5_CHUNK, S5_WIDTH), lambda c: (0, nc - 1 - c, 0))
    yf, yb = pl.pallas_call(
        _s5_scan_kernel,
        grid=(nc,),
        in_specs=[fwd, bwd, _resident(perm.shape), _resident(perm.shape), _resident(wb.shape),
                  _resident(wq.shape), _resident(wt.shape), _resident(a_re.shape), _resident(a_im.shape)],
        out_specs=[fwd, bwd],
        out_shape=[jax.ShapeDtypeStruct((N_BATCH, seq_len, S5_WIDTH), F32)] * 2,
        scratch_shapes=[pltpu.VMEM((rows, S5_WIDTH), F32),
                        pltpu.VMEM((brows, S5_BLOCK * S5_WIDTH), F32),
                        pltpu.VMEM((brows + N_BATCH, S5_NSTATE), F32),
                        pltpu.VMEM((brows + N_BATCH, S5_NSTATE), F32),
                        pltpu.VMEM((rows, S5_WIDTH), F32),
                        pltpu.VMEM((2, 2, N_BATCH, S5_NSTATE), F32)],
        compiler_params=_cparams(("arbitrary",)),
        name="s5_scan",
    )(ua, ua, perm, perm.T, wb, wq, wt, a_re, a_im)
    n = N_BATCH * seq_len
    return yf.reshape(n, S5_WIDTH), yb.reshape(n, S5_WIDTH)


def _mix_out(x, yf_ref, ybk_ref, ua_ref, sg_ref, d_ref, wglu_ref, woa_ref, wob_ref):
    y = yf_ref[...] + ybk_ref[...] + d_ref[...] * ua_ref[...]
    g = _gelu(y)
    ya = g * _sigmoid(_dot(g.astype(BF16), wglu_ref[...]))
    return x + _dot(ya.astype(BF16), woa_ref[...]) + _dot(sg_ref[...], wob_ref[...])


def _pool(x, xp_ref, xn_ref, g_ref, pw_ref, ps_ref, hp_ref, sa_ref, sb_ref, *, seq_len):
    tm = x.shape[0]
    halo = POOL_HALO
    tiles_per_seq = seq_len // tm
    it = pl.program_id(0) % tiles_per_seq
    g = g_ref[...]
    h = _rms(x, g)
    keep_prev = (it > 0).astype(F32)
    keep_next = (it < tiles_per_seq - 1).astype(F32)
    hp_ref[0:halo, :] = _rms(xp_ref[...], g) * keep_prev
    hp_ref[halo:halo + tm, :] = h
    hp_ref[halo + tm:, :] = _rms(xn_ref[...], g) * keep_next
    lo, n = halo // 2, tm + halo
    tail = slice(lo + n, tm + 2 * halo)
    sa_ref[tail, :] = jnp.zeros((halo // 2, D_MODEL), F32)
    sb_ref[tail, :] = jnp.zeros((halo // 2, D_MODEL), F32)

    def doubled(src_ref, dst_ref, step, cols):
        dst_ref[lo:lo + n, cols] = src_ref[lo:lo + n, cols] + src_ref[lo + step:lo + step + n, cols]

    def window(ref, first, cols):
        return ref[first:first + tm, cols] + ref[halo:halo + tm, cols]

    t = it * tm + lax.broadcasted_iota(jnp.int32, (tm, 1), 0)
    outs = []
    for gi, w in enumerate(POOL_WINDOWS):
        cols = slice(gi * POOL_GROUP_DIM, (gi + 1) * POOL_GROUP_DIM)
        if w == 2:
            s = window(hp_ref, halo - 1, cols)
        elif w == 4:
            doubled(hp_ref, sa_ref, 1, cols)
            s = window(sa_ref, halo - 2, cols)
        elif w == 8:
            doubled(hp_ref, sa_ref, 1, cols)
            doubled(sa_ref, sb_ref, 2, cols)
            s = window(sb_ref, halo - 4, cols)
        else:
            assert w == 16
            doubled(hp_ref, sa_ref, 1, cols)
            doubled(sa_ref, sb_ref, 2, cols)
            doubled(sb_ref, sa_ref, 4, cols)
            s = window(sa_ref, halo - 8, cols)
        lo_t = jnp.clip(t - w // 2, 0, seq_len)
        hi_t = jnp.clip(t + w // 2, 0, seq_len)
        cnt = (hi_t - lo_t).astype(F32)
        pooled = s / cnt - h[:, cols]
        outs.append(_dot(pooled.astype(BF16), pw_ref[gi]))
    return x + jnp.concatenate(outs, axis=-1) * ps_ref[...]


def _kv_kernel(m_ref, g_ref, wkv_ref, kt_ref, v_ref):
    hm = _rms(m_ref[...], g_ref[...]).astype(BF16)
    kv = _dot(hm, wkv_ref[...])
    kt_ref[...] = kv[:, :D_MODEL].T.astype(BF16)
    v_ref[...] = kv[:, D_MODEL:].astype(BF16)


def _kv(mem, g, wkv):
    return pl.pallas_call(
        _kv_kernel,
        grid=(N_BATCH,),
        in_specs=[pl.BlockSpec((None, N_MEM, D_MODEL), lambda b: (b, 0, 0)),
                  _resident((1, D_MODEL)), _resident(wkv.shape)],
        out_specs=[pl.BlockSpec((None, D_MODEL, N_MEM), lambda b: (b, 0, 0)),
                   pl.BlockSpec((None, N_MEM, D_MODEL), lambda b: (b, 0, 0))],
        out_shape=[jax.ShapeDtypeStruct((N_BATCH, D_MODEL, N_MEM), BF16),
                   jax.ShapeDtypeStruct((N_BATCH, N_MEM, D_MODEL), BF16)],
        compiler_params=_cparams(("arbitrary",)),
        name="mem_kv",
    )(mem, g, wkv)


def _cross_attn(x, g_ref, wq_ref, kt_ref, v_ref, wo_ref):
    h = _rms(x, g_ref[...]).astype(BF16)
    q = (_dot(h, wq_ref[...]) * (CROSS_HEAD_DIM ** -0.5)).astype(BF16)
    heads = []
    for hd in range(CROSS_HEADS):
        cols = slice(hd * CROSS_HEAD_DIM, (hd + 1) * CROSS_HEAD_DIM)
        s = _dot(q[:, cols], kt_ref[cols, :])
        e = jnp.exp(s - jnp.max(s, axis=-1, keepdims=True))
        p = e * (1.0 / jnp.sum(e, axis=-1, keepdims=True))
        heads.append(_dot(p.astype(BF16), v_ref[:, cols]).astype(BF16))
    o = jnp.concatenate(heads, axis=-1)
    return x + _dot(o, wo_ref[...])


def _mix_cross_kernel(x_ref, yf_ref, ybk_ref, ua_ref, sg_ref, d_ref, wglu_ref, woa_ref, wob_ref,
                      g_ref, wq_ref, kt_ref, v_ref, wo_ref, o_ref):
    x = _mix_out(x_ref[...], yf_ref, ybk_ref, ua_ref, sg_ref, d_ref, wglu_ref, woa_ref, wob_ref)
    o_ref[...] = _cross_attn(x, g_ref, wq_ref, kt_ref, v_ref, wo_ref)


def _pool_cross_kernel(x_ref, xp_ref, xn_ref, mg_ref, pw_ref, ps_ref,
                       g_ref, wq_ref, kt_ref, v_ref, wo_ref, o_ref, hp_ref, sa_ref, sb_ref, *, seq_len):
    x = _pool(x_ref[...], xp_ref, xn_ref, mg_ref, pw_ref, ps_ref, hp_ref, sa_ref, sb_ref, seq_len=seq_len)
    o_ref[...] = _cross_attn(x, g_ref, wq_ref, kt_ref, v_ref, wo_ref)


def _cross_specs(seq_len, tm, wq, wo):
    tiles_per_seq = seq_len // tm
    return [_resident((1, D_MODEL)), _resident(wq.shape),
            pl.BlockSpec((None, D_MODEL, N_MEM), lambda i: (i // tiles_per_seq, 0, 0)),
            pl.BlockSpec((None, N_MEM, D_MODEL), lambda i: (i // tiles_per_seq, 0, 0)),
            _resident(wo.shape)]


def _mix_cross(x, seq_len, yf, ybk, ua, sg, d, wglu, woa, wob, g, wq, kt, v, wo):
    tm = TOK_ROWS
    n = x.shape[0]
    return pl.pallas_call(
        _mix_cross_kernel,
        grid=(n // tm,),
        in_specs=[_row_spec(tm, D_MODEL), _row_spec(tm, S5_WIDTH), _row_spec(tm, S5_WIDTH),
                  _row_spec(tm, S5_WIDTH), _row_spec(tm, SGU_WIDTH),
                  _resident((1, S5_WIDTH)), _resident(wglu.shape), _resident(woa.shape),
                  _resident(wob.shape)] + _cross_specs(seq_len, tm, wq, wo),
        out_specs=_row_spec(tm, D_MODEL),
        out_shape=jax.ShapeDtypeStruct((n, D_MODEL), F32),
        compiler_params=_cparams(("arbitrary",)),
        name="mix_cross",
    )(x, yf, ybk, ua, sg, d, wglu, woa, wob, g, wq, kt, v, wo)


def _pool_cross(x, seq_len, mg, pw, ps, g, wq, kt, v, wo):
    tm = TOK_ROWS
    n = x.shape[0]
    per = tm // POOL_HALO
    last = n // POOL_HALO - 1
    prev = pl.BlockSpec((POOL_HALO, D_MODEL), lambda i: (jnp.maximum(i * per - 1, 0), 0))
    nxt = pl.BlockSpec((POOL_HALO, D_MODEL), lambda i: (jnp.minimum((i + 1) * per, last), 0))
    padded = pltpu.VMEM((tm + 2 * POOL_HALO, D_MODEL), F32)
    return pl.pallas_call(
        functools.partial(_pool_cross_kernel, seq_len=seq_len),
        grid=(n // tm,),
        in_specs=[_row_spec(tm, D_MODEL), prev, nxt, _resident((1, D_MODEL)), _resident(pw.shape),
                  _resident((1, D_MODEL))] + _cross_specs(seq_len, tm, wq, wo),
        out_specs=_row_spec(tm, D_MODEL),
        out_shape=jax.ShapeDtypeStruct((n, D_MODEL), F32),
        scratch_shapes=[padded, padded, padded],
        compiler_params=_cparams(("arbitrary",)),
        name="pool_cross",
    )(x, x, x, mg, pw, ps, g, wq, kt, v, wo)


def _trunk(x, mem, p):
    n_batch, seq_len, _ = x.shape
    assert n_batch == N_BATCH and seq_len % max(FFN_ROWS, TOK_ROWS) == 0 and seq_len % S5_CHUNK == 0
    x = x.reshape(n_batch * seq_len, D_MODEL)
    row = lambda a: a.reshape(1, -1)
    fg = row(p['final_norm'])
    for i in range(2):
        x = _ffn(x, row(p['ffn1_norm'][i]), p['ffn1_w_gate'][i], p['ffn1_w_up'][i],
                 p['ffn1_w_down'][i], fg, name=f"ffn1_l{i}")
        mg = row(p['mix_norm'][i])
        kt, v = _kv(mem, row(p['mem_norm'][i]), p['cross_w_kv'][i])
        cross = (row(p['cross_norm'][i]), p['cross_w_q'][i], kt, v, p['cross_w_o'][i])
        if i == 0:
            ua, sg = _inproj_sgu(x, mg, p['ab_w_in'], row(p['sgu_norm_g']), row(p['sgu_norm_b']),
                                 p['sgu_w_s'], p['sgu_b_s'])
            yf, ybk = _s5_scan(ua, seq_len, p['s5_wb'], p['s5_wq'], p['s5_wt'], p['s5_a_re'], p['s5_a_im'])
            x = _mix_cross(x, seq_len, yf, ybk, ua, sg, row(p['s5_d']), p['s5_w_glu'],
                           p['ab_w_out_a'], p['ab_w_out_b'], *cross)
        else:
            x = _pool_cross(x, seq_len, mg, p['pool_w'], row(p['pool_scale']), *cross)
        x = _ffn(x, row(p['ffn2_norm'][i]), p['ffn2_w_gate'][i], p['ffn2_w_up'][i],
                 p['ffn2_w_down'][i], fg, final_norm=(i == 1), name=f"ffn2_l{i}")
    return x.reshape(n_batch, seq_len, D_MODEL)


def kernel(x_prompt, x_sample, mem_prompt, mem_sample, ffn1_norm, ffn1_w_gate, ffn1_w_up, ffn1_w_down, mix_norm, ab_w_in, s5_lambda_re, s5_lambda_im, s5_log_dt, s5_b_re, s5_b_im, s5_c_re, s5_c_im, s5_d, s5_w_glu, sgu_norm_g, sgu_norm_b, sgu_w_s, sgu_b_s, ab_w_out, pool_w, pool_scale, cross_norm, mem_norm, cross_w_q, cross_w_kv, cross_w_o, ffn2_norm, ffn2_w_gate, ffn2_w_up, ffn2_w_down, final_norm):
    bf = lambda a: a.astype(BF16)
    ar_re, ar_im, wb, wq, wt = _s5_prepare(
        s5_lambda_re[0], s5_lambda_im[0], s5_log_dt[0], s5_b_re[0], s5_b_im[0], s5_c_re[0], s5_c_im[0])
    p = dict(
        ffn1_norm=ffn1_norm, ffn1_w_gate=bf(ffn1_w_gate), ffn1_w_up=bf(ffn1_w_up),
        ffn1_w_down=bf(ffn1_w_down), mix_norm=mix_norm, ab_w_in=bf(ab_w_in[0]),
        s5_wb=wb, s5_wq=wq, s5_wt=wt, s5_a_re=ar_re, s5_a_im=ar_im,
        s5_d=s5_d[0], s5_w_glu=bf(s5_w_glu[0]),
        sgu_norm_g=sgu_norm_g[0], sgu_norm_b=sgu_norm_b[0], sgu_w_s=bf(sgu_w_s[0]),
        sgu_b_s=jnp.broadcast_to(sgu_b_s[0][:, :, None], (SGU_HEADS, SGU_CHUNK, SGU_HEAD_DIM)),
        ab_w_out_a=bf(ab_w_out[0, :S5_WIDTH]), ab_w_out_b=bf(ab_w_out[0, S5_WIDTH:]),
        pool_w=bf(pool_w[0]), pool_scale=pool_scale[0],
        cross_norm=cross_norm, mem_norm=mem_norm, cross_w_q=bf(cross_w_q), cross_w_kv=bf(cross_w_kv),
        cross_w_o=bf(cross_w_o), ffn2_norm=ffn2_norm, ffn2_w_gate=bf(ffn2_w_gate),
        ffn2_w_up=bf(ffn2_w_up), ffn2_w_down=bf(ffn2_w_down), final_norm=final_norm)
    return (_trunk(x_prompt, mem_prompt, p), _trunk(x_sample, mem_sample, p))
```

```python
import functools
import math

import jax
import jax.numpy as jnp
from jax import lax
from jax.experimental import pallas as pl
from jax.experimental.pallas import tpu as pltpu

F32 = jnp.float32
BF16 = jnp.bfloat16

D_MODEL = 1024
D_FF = 4 * D_MODEL
N_BATCH = 8
N_MEM = 256
CROSS_HEADS = 4
CROSS_HEAD_DIM = D_MODEL // CROSS_HEADS
S5_WIDTH = D_MODEL // 2
S5_GROUP = 16
S5_GROUPS = S5_WIDTH // S5_GROUP
S5_STATE = 64
S5_NSTATE = S5_GROUPS * S5_STATE
SGU_WIDTH = D_MODEL // 2
SGU_HEADS = 4
SGU_HEAD_DIM = SGU_WIDTH // SGU_HEADS
SGU_CHUNK = 128
POOL_WINDOWS = (2, 4, 8, 16)
POOL_GROUP_DIM = D_MODEL // len(POOL_WINDOWS)
POOL_HALO = 16
EPS = 1e-6

V7X_LANES = 128
V7X_SUBLANES = 8
V7X_VMEM_BYTES = 64 * 1024 * 1024
VMEM_LIMIT_BYTES = 56 * 1024 * 1024

FFN_ROWS = 1024
FFN_SUB_ROWS = 512
FFN_FCHUNK = 512
TOK_ROWS = 1024
S5_CHUNK = 64
S5_BLOCK = 4
S5_PERM_STEPS = 32
S5_COLS = 512
S5_BTILE = 256
S5_CTILE = 512
S5_CH_BLOCKS = S5_WIDTH // V7X_LANES


def _cparams(sem):
    return pltpu.CompilerParams(dimension_semantics=sem, vmem_limit_bytes=VMEM_LIMIT_BYTES)


def _resident(shape):
    nd = len(shape)
    return pl.BlockSpec(shape, lambda *_: (0,) * nd, pipeline_mode=pl.Buffered(1))


def _row_spec(tm, width):
    return pl.BlockSpec((tm, width), lambda i: (i, 0))


def _rms(x, g):
    ms = jnp.mean(x * x, axis=-1, keepdims=True)
    return x * lax.rsqrt(ms + EPS) * g


def _gelu(x):
    c = math.sqrt(2.0 / math.pi)
    return x * (0.5 * (1.0 + jnp.tanh(c * (x + 0.044715 * (x * x * x)))))


def _sigmoid(x):
    return 1.0 / (1.0 + jnp.exp(-x))


def _dot(a, b):
    return jnp.dot(a, b, preferred_element_type=F32)


def _ffn_kernel(x_ref, g_ref, wg_ref, wu_ref, wd_ref, fg_ref, o_ref, *, final_norm):
    for r0 in range(0, x_ref.shape[0], FFN_SUB_ROWS):
        rows = slice(r0, r0 + FFN_SUB_ROWS)
        x = x_ref[rows, :]
        h = _rms(x, g_ref[...]).astype(BF16)
        acc = jnp.zeros(x.shape, F32)
        for c in range(D_FF // FFN_FCHUNK):
            cols = slice(c * FFN_FCHUNK, (c + 1) * FFN_FCHUNK)
            gate = _dot(h, wg_ref[:, cols])
            up = _dot(h, wu_ref[:, cols])
            a = (gate * _sigmoid(gate) * up).astype(BF16)
            acc = acc + _dot(a, wd_ref[cols, :])
        y = x + 0.5 * acc
        if final_norm:
            y = _rms(y, fg_ref[...])
        o_ref[rows, :] = y


def _ffn(x, g, wg, wu, wd, fg, *, final_norm=False, name):
    tm = FFN_ROWS
    n = x.shape[0]
    return pl.pallas_call(
        functools.partial(_ffn_kernel, final_norm=final_norm),
        grid=(n // tm,),
        in_specs=[_row_spec(tm, D_MODEL), _resident((1, D_MODEL)), _resident((D_MODEL, D_FF)),
                  _resident((D_MODEL, D_FF)), _resident((D_FF, D_MODEL)), _resident((1, D_MODEL))],
        out_specs=_row_spec(tm, D_MODEL),
        out_shape=jax.ShapeDtypeStruct((n, D_MODEL), F32),
        compiler_params=_cparams(("arbitrary",)),
        name=name,
    )(x, g, wg, wu, wd, fg)


def _inproj_sgu_kernel(x_ref, g_ref, win_ref, lng_ref, lnb_ref, ws_ref, bs_ref, ua_ref, yb_ref):
    h = _rms(x_ref[...], g_ref[...]).astype(BF16)
    z = _dot(h, win_ref[...])
    ua_ref[...] = z[:, :S5_WIDTH]
    u = _gelu(z[:, S5_WIDTH:S5_WIDTH + SGU_WIDTH])
    v = _gelu(z[:, S5_WIDTH + SGU_WIDTH:])
    mu = jnp.mean(v, axis=-1, keepdims=True)
    vc = v - mu
    var = jnp.mean(vc * vc, axis=-1, keepdims=True)
    v = (vc * lax.rsqrt(var + EPS) * lng_ref[...] + lnb_ref[...]).astype(BF16)
    for n in range(x_ref.shape[0] // SGU_CHUNK):
        rows = slice(n * SGU_CHUNK, (n + 1) * SGU_CHUNK)
        for hd in range(SGU_HEADS):
            cols = slice(hd * SGU_HEAD_DIM, (hd + 1) * SGU_HEAD_DIM)
            vs = _dot(ws_ref[hd], v[rows, cols]) + bs_ref[hd]
            yb_ref[rows, cols] = (u[rows, cols] * vs).astype(BF16)


def _inproj_sgu(x, g, win, lng, lnb, ws, bs):
    tm = TOK_ROWS
    n = x.shape[0]
    return pl.pallas_call(
        _inproj_sgu_kernel,
        grid=(n // tm,),
        in_specs=[_row_spec(tm, D_MODEL), _resident((1, D_MODEL)), _resident(win.shape),
                  _resident((1, SGU_WIDTH)), _resident((1, SGU_WIDTH)),
                  _resident(ws.shape), _resident(bs.shape)],
        out_specs=[_row_spec(tm, S5_WIDTH), _row_spec(tm, SGU_WIDTH)],
        out_shape=[jax.ShapeDtypeStruct((n, S5_WIDTH), F32),
                   jax.ShapeDtypeStruct((n, SGU_WIDTH), BF16)],
        compiler_params=_cparams(("arbitrary",)),
        name="inproj_sgu",
    )(x, g, win, lng, lnb, ws, bs)


def _zoh_a(lam_re, lam_im, log_dt):
    dt = jnp.exp(log_dt)
    mag = jnp.exp(lam_re * dt)
    ang = lam_im * dt
    return mag * jnp.cos(ang), mag * jnp.sin(ang)


def _s5_prep_kernel(lre_ref, lim_ref, ldt_ref, bre_ref, bim_ref, cre_ref, cim_ref,
                    lre3_ref, lim3_ref, ldt3_ref, ctre_ref, ctim_ref,
                    are_ref, aim_ref, wb_ref, wq_ref, wt_ref):
    r, g, grp, lanes = S5_BLOCK, S5_GROUPS, S5_GROUP, V7X_LANES
    lam_re = lre_ref[...]
    lam_im = lim_ref[...]
    a_re, a_im = _zoh_a(lam_re, lam_im, ldt_ref[...])
    nr = a_re - 1.0
    ni = a_im
    den = lam_re * lam_re + lam_im * lam_im
    q_re = ((nr * lam_re + ni * lam_im) / den)[:, None, :]
    q_im = ((ni * lam_re - nr * lam_im) / den)[:, None, :]
    b_re = bre_ref[...]
    b_im = bim_ref[...]
    bb_re = q_re * b_re - q_im * b_im
    bb_im = q_re * b_im + q_im * b_re
    c_re = cre_ref[...]
    c_im = cim_ref[...]
    a3_re, a3_im = _zoh_a(lre3_ref[...], lim3_ref[...], ldt3_ref[...])
    ct_re = ctre_ref[...]
    ct_im = ctim_ref[...]

    bk, ck, kk = [], [], []
    p_re, p_im = jnp.ones_like(a_re), jnp.zeros_like(a_im)
    p3_re, p3_im = jnp.ones_like(a3_re), jnp.zeros_like(a3_im)
    contract_states = (((2,), (2,)), ((0,), (0,)))
    for k in range(r + 1):
        if k < r:
            pr, pi = p_re[:, None, :], p_im[:, None, :]
            bk_re = pr * bb_re - pi * bb_im
            bk_im = pr * bb_im + pi * bb_re
            bk.append((bk_re, bk_im))
            kk.append(lax.dot_general(bk_re, c_re, contract_states, precision=lax.Precision.HIGHEST,
                                      preferred_element_type=F32)
                      - lax.dot_general(bk_im, c_im, contract_states, precision=lax.Precision.HIGHEST,
                                        preferred_element_type=F32))
        if k >= 1:
            ck.append((p3_re * ct_re - p3_im * ct_im, -(p3_re * ct_im + p3_im * ct_re)))
        if k == r:
            are_ref[...] = p_re
            aim_ref[...] = p_im
        p_re, p_im = p_re * a_re - p_im * a_im, p_re * a_im + p_im * a_re
        p3_re, p3_im = p3_re * a3_re - p3_im * a3_im, p3_re * a3_im + p3_im * a3_re

    wb_ref[...] = jnp.zeros(wb_ref.shape, BF16)
    wq_ref[...] = jnp.zeros(wq_ref.shape, BF16)
    wt_ref[...] = jnp.zeros(wt_ref.shape, BF16)
    groups_per_btile = S5_BTILE // S5_STATE
    groups_per_lanes = lanes // grp
    for d in range(2):
        k_in = [r - 1 - s if d == 0 else s for s in range(r)]
        k_out = [s + 1 if d == 0 else r - s for s in range(r)]
        for j in range(S5_NSTATE // S5_BTILE):
            for gt in range(groups_per_btile):
                gi = j * groups_per_btile + gt
                gl = gi % groups_per_lanes
                for s in range(r):
                    rows = slice(s * lanes + gl * grp, s * lanes + (gl + 1) * grp)
                    cols = slice(gt * S5_STATE, (gt + 1) * S5_STATE)
                    for part in range(2):
                        wb_ref[d, part, j, rows, cols] = bk[k_in[s]][part][d * g + gi].astype(BF16)
        for m in range(S5_CH_BLOCKS):
            for gl in range(groups_per_lanes):
                gi = m * groups_per_lanes + gl
                srows = slice(gl * S5_STATE, (gl + 1) * S5_STATE)
                for s in range(r):
                    cols = slice(s * lanes + gl * grp, s * lanes + (gl + 1) * grp)
                    for part in range(2):
                        wq_ref[d, part, m, srows, cols] = ck[k_out[s] - 1][part][d * g + gi].astype(BF16)
                    for s_out in range(r):
                        lag = (s_out - s) if d == 0 else (s - s_out)
                        if lag >= 0:
                            ocols = slice(s_out * lanes + gl * grp, s_out * lanes + (gl + 1) * grp)
                            wt_ref[d, m, cols, ocols] = kk[lag][d * g + gi].astype(BF16)


def _s5_prepare(lam_re, lam_im, log_dt, b_re, b_im, c_re, c_im):
    g2 = 2 * S5_GROUPS
    r = S5_BLOCK
    rl = r * V7X_LANES
    flat = lambda a: a.reshape(g2, S5_STATE)
    gcp = lambda a: a.reshape(g2, S5_GROUP, S5_STATE)
    gpc = lambda a: jnp.swapaxes(gcp(a), -1, -2)
    on_sublanes = lambda a: jnp.broadcast_to(a.reshape(g2, -1, 1), (g2, S5_STATE, S5_GROUP))
    a_re, a_im, wb, wq, wt = pl.pallas_call(
        _s5_prep_kernel,
        out_shape=[jax.ShapeDtypeStruct((g2, S5_STATE), F32)] * 2
        + [jax.ShapeDtypeStruct((2, 2, S5_NSTATE // S5_BTILE, rl, S5_BTILE), BF16),
           jax.ShapeDtypeStruct((2, 2, S5_CH_BLOCKS, S5_CTILE, rl), BF16),
           jax.ShapeDtypeStruct((2, S5_CH_BLOCKS, rl, rl), BF16)],
        compiler_params=pltpu.CompilerParams(vmem_limit_bytes=VMEM_LIMIT_BYTES),
        name="s5_prepare",
    )(flat(lam_re), flat(lam_im), log_dt.reshape(g2, 1),
      gcp(jnp.swapaxes(b_re, -1, -2)), gcp(jnp.swapaxes(b_im, -1, -2)), gcp(c_re), gcp(c_im),
      on_sublanes(lam_re), on_sublanes(lam_im), on_sublanes(log_dt), gpc(c_re), gpc(c_im))
    return a_re.reshape(2, S5_NSTATE), a_im.reshape(2, S5_NSTATE), wb, wq, wt


def _s5_scan_kernel(uf_ref, ub_ref, perm_ref, permt_ref, wb_ref, wq_ref, wt_ref, are_ref, aim_ref,
                    yf_ref, yb_ref, utb_ref, up_ref, sre_ref, sim_ref, ytb_ref, st_ref):
    lc, r, nb, lanes = S5_CHUNK, S5_BLOCK, N_BATCH, V7X_LANES
    nblk = lc // r
    brows = nblk * nb
    prow = S5_PERM_STEPS * nb

    @pl.when(pl.program_id(0) == 0)
    def _():
        st_ref[...] = jnp.zeros(st_ref.shape, F32)

    for d, (u_ref, y_ref) in enumerate(((uf_ref, yf_ref), (ub_ref, yb_ref))):
        for h in range(lc // S5_PERM_STEPS):
            blk = u_ref[:, h * S5_PERM_STEPS:(h + 1) * S5_PERM_STEPS, :].reshape(prow, S5_WIDTH)
            utb_ref[h * prow:(h + 1) * prow, :] = _dot(perm_ref[...], blk.astype(BF16))
        for j in range(nblk):
            for s in range(r):
                slab = utb_ref[(j * r + s) * nb:(j * r + s + 1) * nb, :]
                for kb in range(S5_CH_BLOCKS):
                    up_ref[j * nb:(j + 1) * nb, (kb * r + s) * lanes:(kb * r + s + 1) * lanes] = (
                        slab[:, kb * lanes:(kb + 1) * lanes])

        off = nb if d == 0 else 0
        carry = 0 if d == 0 else brows
        ch_per_btile = S5_BTILE // S5_STATE * S5_GROUP
        for j in range(S5_NSTATE // S5_BTILE):
            kb = (j * ch_per_btile) // lanes
            cols = slice(j * S5_BTILE, (j + 1) * S5_BTILE)
            lhs = up_ref[:, kb * r * lanes:(kb + 1) * r * lanes].astype(BF16)
            sre_ref[off:off + brows, cols] = _dot(lhs, wb_ref[d, 0, j])
            sim_ref[off:off + brows, cols] = _dot(lhs, wb_ref[d, 1, j])

        for cb in range(S5_NSTATE // S5_COLS):
            cols = slice(cb * S5_COLS, (cb + 1) * S5_COLS)
            a_re = jnp.broadcast_to(are_ref[d:d + 1, cols], (nb, S5_COLS))
            a_im = jnp.broadcast_to(aim_ref[d:d + 1, cols], (nb, S5_COLS))
            h_re = st_ref[d, 0, :, cols]
            h_im = st_ref[d, 1, :, cols]
            sre_ref[carry:carry + nb, cols] = h_re
            sim_ref[carry:carry + nb, cols] = h_im
            for jj in range(nblk):
                j = jj if d == 0 else nblk - 1 - jj
                rows = slice(off + j * nb, off + (j + 1) * nb)
                h_re, h_im = (a_re * h_re - a_im * h_im + sre_ref[rows, cols],
                              a_re * h_im + a_im * h_re + sim_ref[rows, cols])
                sre_ref[rows, cols] = h_re
                sim_ref[rows, cols] = h_im
            st_ref[d, 0, :, cols] = h_re
            st_ref[d, 1, :, cols] = h_im

        enter = slice(0, brows) if d == 0 else slice(nb, brows + nb)
        for m in range(S5_CH_BLOCKS):
            srows = slice(m * S5_CTILE, (m + 1) * S5_CTILE)
            ym = (_dot(sre_ref[enter, srows].astype(BF16), wq_ref[d, 0, m])
                  + _dot(sim_ref[enter, srows].astype(BF16), wq_ref[d, 1, m])
                  + _dot(up_ref[:, m * r * lanes:(m + 1) * r * lanes].astype(BF16), wt_ref[d, m]))
            for j in range(nblk):
                for s in range(r):
                    ytb_ref[(j * r + s) * nb:(j * r + s + 1) * nb, m * lanes:(m + 1) * lanes] = (
                        ym[j * nb:(j + 1) * nb, s * lanes:(s + 1) * lanes])

        for h in range(lc // S5_PERM_STEPS):
            y = ytb_ref[h * prow:(h + 1) * prow, :]
            y_hi = y.astype(BF16)
            y_lo = (y - y_hi.astype(F32)).astype(BF16)
            y = _dot(permt_ref[...], y_hi) + _dot(permt_ref[...], y_lo)
            y_ref[:, h * S5_PERM_STEPS:(h + 1) * S5_PERM_STEPS, :] = y.reshape(nb, S5_PERM_STEPS, S5_WIDTH)


def _time_major_permutation():
    n = S5_PERM_STEPS * N_BATCH
    r = jnp.arange(n)
    src = (r % N_BATCH) * S5_PERM_STEPS + r // N_BATCH
    return jax.nn.one_hot(src, n, dtype=BF16)


def _s5_scan(ua, seq_len, wb, wq, wt, a_re, a_im):
    nc = seq_len // S5_CHUNK
    rows = S5_CHUNK * N_BATCH
    brows = rows // S5_BLOCK
    ua = ua.reshape(N_BATCH, seq_len, S5_WIDTH)
    perm = _time_major_permutation()
    fwd = pl.BlockSpec((N_BATCH, S5_CHUNK, S5_WIDTH), lambda c: (0, c, 0))
    bwd = pl.BlockSpec((N_BATCH, S5_CHUNK, S5_WIDTH), lambda c: (0, nc - 1 - c, 0))
    yf, yb = pl.pallas_call(
        _s5_scan_kernel,
        grid=(nc,),
        in_specs=[fwd, bwd, _resident(perm.shape), _resident(perm.shape), _resident(wb.shape),
                  _resident(wq.shape), _resident(wt.shape), _resident(a_re.shape), _resident(a_im.shape)],
        out_specs=[fwd, bwd],
        out_shape=[jax.ShapeDtypeStruct((N_BATCH, seq_len, S5_WIDTH), F32)] * 2,
        scratch_shapes=[pltpu.VMEM((rows, S5_WIDTH), F32),
                        pltpu.VMEM((brows, S5_BLOCK * S5_WIDTH), F32),
                        pltpu.VMEM((brows + N_BATCH, S5_NSTATE), F32),
                        pltpu.VMEM((brows + N_BATCH, S5_NSTATE), F32),
                        pltpu.VMEM((rows, S5_WIDTH), F32),
                        pltpu.VMEM((2, 2, N_BATCH, S5_NSTATE), F32)],
        compiler_params=_cparams(("arbitrary",)),
        name="s5_scan",
    )(ua, ua, perm, perm.T, wb, wq, wt, a_re, a_im)
    n = N_BATCH * seq_len
    return yf.reshape(n, S5_WIDTH), yb.reshape(n, S5_WIDTH)


def _mix_out(x, yf_ref, ybk_ref, ua_ref, sg_ref, d_ref, wglu_ref, woa_ref, wob_ref):
    y = yf_ref[...] + ybk_ref[...] + d_ref[...] * ua_ref[...]
    g = _gelu(y)
    ya = g * _sigmoid(_dot(g.astype(BF16), wglu_ref[...]))
    return x + _dot(ya.astype(BF16), woa_ref[...]) + _dot(sg_ref[...], wob_ref[...])


def _pool(x, xp_ref, xn_ref, g_ref, pw_ref, ps_ref, hp_ref, sa_ref, sb_ref, *, seq_len):
    tm = x.shape[0]
    halo = POOL_HALO
    tiles_per_seq = seq_len // tm
    it = pl.program_id(0) % tiles_per_seq
    g = g_ref[...]
    h = _rms(x, g)
    keep_prev = (it > 0).astype(F32)
    keep_next = (it < tiles_per_seq - 1).astype(F32)
    hp_ref[0:halo, :] = _rms(xp_ref[...], g) * keep_prev
    hp_ref[halo:halo + tm, :] = h
    hp_ref[halo + tm:, :] = _rms(xn_ref[...], g) * keep_next
    lo, n = halo // 2, tm + halo
    tail = slice(lo + n, tm + 2 * halo)
    sa_ref[tail, :] = jnp.zeros((halo // 2, D_MODEL), F32)
    sb_ref[tail, :] = jnp.zeros((halo // 2, D_MODEL), F32)

    def doubled(src_ref, dst_ref, step, cols):
        dst_ref[lo:lo + n, cols] = src_ref[lo:lo + n, cols] + src_ref[lo + step:lo + step + n, cols]

    def window(ref, first, cols):
        return ref[first:first + tm, cols] + ref[halo:halo + tm, cols]

    t = it * tm + lax.broadcasted_iota(jnp.int32, (tm, 1), 0)
    outs = []
    for gi, w in enumerate(POOL_WINDOWS):
        cols = slice(gi * POOL_GROUP_DIM, (gi + 1) * POOL_GROUP_DIM)
        if w == 2:
            s = window(hp_ref, halo - 1, cols)
        elif w == 4:
            doubled(hp_ref, sa_ref, 1, cols)
            s = window(sa_ref, halo - 2, cols)
        elif w == 8:
            doubled(hp_ref, sa_ref, 1, cols)
            doubled(sa_ref, sb_ref, 2, cols)
            s = window(sb_ref, halo - 4, cols)
        else:
            assert w == 16
            doubled(hp_ref, sa_ref, 1, cols)
            doubled(sa_ref, sb_ref, 2, cols)
            doubled(sb_ref, sa_ref, 4, cols)
            s = window(sa_ref, halo - 8, cols)
        lo_t = jnp.clip(t - w // 2, 0, seq_len)
        hi_t = jnp.clip(t + w // 2, 0, seq_len)
        inv_cnt = 1.0 / (hi_t - lo_t).astype(F32)
        pooled = s * inv_cnt - h[:, cols]
        outs.append(_dot(pooled.astype(BF16), pw_ref[gi]))
    return x + jnp.concatenate(outs, axis=-1) * ps_ref[...]


def _kv_kernel(m_ref, g_ref, wkv_ref, kt_ref, v_ref):
    hm = _rms(m_ref[...], g_ref[...]).astype(BF16)
    kv = _dot(hm, wkv_ref[...])
    kt_ref[...] = kv[:, :D_MODEL].T.astype(BF16)
    v_ref[...] = kv[:, D_MODEL:].astype(BF16)


def _kv(mem, g, wkv):
    return pl.pallas_call(
        _kv_kernel,
        grid=(N_BATCH,),
        in_specs=[pl.BlockSpec((None, N_MEM, D_MODEL), lambda b: (b, 0, 0)),
                  _resident((1, D_MODEL)), _resident(wkv.shape)],
        out_specs=[pl.BlockSpec((None, D_MODEL, N_MEM), lambda b: (b, 0, 0)),
                   pl.BlockSpec((None, N_MEM, D_MODEL), lambda b: (b, 0, 0))],
        out_shape=[jax.ShapeDtypeStruct((N_BATCH, D_MODEL, N_MEM), BF16),
                   jax.ShapeDtypeStruct((N_BATCH, N_MEM, D_MODEL), BF16)],
        compiler_params=_cparams(("arbitrary",)),
        name="mem_kv",
    )(mem, g, wkv)


def _cross_attn(x, g_ref, wq_ref, kt_ref, v_ref, wo_ref):
    h = _rms(x, g_ref[...]).astype(BF16)
    q = (_dot(h, wq_ref[...]) * (CROSS_HEAD_DIM ** -0.5)).astype(BF16)
    heads = []
    for hd in range(CROSS_HEADS):
        cols = slice(hd * CROSS_HEAD_DIM, (hd + 1) * CROSS_HEAD_DIM)
        s = _dot(q[:, cols], kt_ref[cols, :])
        e = jnp.exp(s - jnp.max(s, axis=-1, keepdims=True))
        p = e * (1.0 / jnp.sum(e, axis=-1, keepdims=True))
        heads.append(_dot(p.astype(BF16), v_ref[:, cols]).astype(BF16))
    o = jnp.concatenate(heads, axis=-1)
    return x + _dot(o, wo_ref[...])


def _mix_cross_kernel(x_ref, yf_ref, ybk_ref, ua_ref, sg_ref, d_ref, wglu_ref, woa_ref, wob_ref,
                      g_ref, wq_ref, kt_ref, v_ref, wo_ref, o_ref):
    x = _mix_out(x_ref[...], yf_ref, ybk_ref, ua_ref, sg_ref, d_ref, wglu_ref, woa_ref, wob_ref)
    o_ref[...] = _cross_attn(x, g_ref, wq_ref, kt_ref, v_ref, wo_ref)


def _pool_cross_kernel(x_ref, xp_ref, xn_ref, mg_ref, pw_ref, ps_ref,
                       g_ref, wq_ref, kt_ref, v_ref, wo_ref, o_ref, hp_ref, sa_ref, sb_ref, *, seq_len):
    x = _pool(x_ref[...], xp_ref, xn_ref, mg_ref, pw_ref, ps_ref, hp_ref, sa_ref, sb_ref, seq_len=seq_len)
    o_ref[...] = _cross_attn(x, g_ref, wq_ref, kt_ref, v_ref, wo_ref)


def _cross_specs(seq_len, tm, wq, wo):
    tiles_per_seq = seq_len // tm
    return [_resident((1, D_MODEL)), _resident(wq.shape),
            pl.BlockSpec((None, D_MODEL, N_MEM), lambda i: (i // tiles_per_seq, 0, 0)),
            pl.BlockSpec((None, N_MEM, D_MODEL), lambda i: (i // tiles_per_seq, 0, 0)),
            _resident(wo.shape)]


def _mix_cross(x, seq_len, yf, ybk, ua, sg, d, wglu, woa, wob, g, wq, kt, v, wo):
    tm = TOK_ROWS
    n = x.shape[0]
    return pl.pallas_call(
        _mix_cross_kernel,
        grid=(n // tm,),
        in_specs=[_row_spec(tm, D_MODEL), _row_spec(tm, S5_WIDTH), _row_spec(tm, S5_WIDTH),
                  _row_spec(tm, S5_WIDTH), _row_spec(tm, SGU_WIDTH),
                  _resident((1, S5_WIDTH)), _resident(wglu.shape), _resident(woa.shape),
                  _resident(wob.shape)] + _cross_specs(seq_len, tm, wq, wo),
        out_specs=_row_spec(tm, D_MODEL),
        out_shape=jax.ShapeDtypeStruct((n, D_MODEL), F32),
        compiler_params=_cparams(("arbitrary",)),
        name="mix_cross",
    )(x, yf, ybk, ua, sg, d, wglu, woa, wob, g, wq, kt, v, wo)


def _pool_cross(x, seq_len, mg, pw, ps, g, wq, kt, v, wo):
    tm = TOK_ROWS
    n = x.shape[0]
    per = tm // POOL_HALO
    last = n // POOL_HALO - 1
    prev = pl.BlockSpec((POOL_HALO, D_MODEL), lambda i: (jnp.maximum(i * per - 1, 0), 0))
    nxt = pl.BlockSpec((POOL_HALO, D_MODEL), lambda i: (jnp.minimum((i + 1) * per, last), 0))
    padded = pltpu.VMEM((tm + 2 * POOL_HALO, D_MODEL), F32)
    return pl.pallas_call(
        functools.partial(_pool_cross_kernel, seq_len=seq_len),
        grid=(n // tm,),
        in_specs=[_row_spec(tm, D_MODEL), prev, nxt, _resident((1, D_MODEL)), _resident(pw.shape),
                  _resident((1, D_MODEL))] + _cross_specs(seq_len, tm, wq, wo),
        out_specs=_row_spec(tm, D_MODEL),
        out_shape=jax.ShapeDtypeStruct((n, D_MODEL), F32),
        scratch_shapes=[padded, padded, padded],
        compiler_params=_cparams(("arbitrary",)),
        name="pool_cross",
    )(x, x, x, mg, pw, ps, g, wq, kt, v, wo)


def _trunk(x, mem, p):
    n_batch, seq_len, _ = x.shape
    assert n_batch == N_BATCH and seq_len % max(FFN_ROWS, TOK_ROWS) == 0 and seq_len % S5_CHUNK == 0
    x = x.reshape(n_batch * seq_len, D_MODEL)
    row = lambda a: a.reshape(1, -1)
    fg = row(p['final_norm'])
    for i in range(2):
        x = _ffn(x, row(p['ffn1_norm'][i]), p['ffn1_w_gate'][i], p['ffn1_w_up'][i],
                 p['ffn1_w_down'][i], fg, name=f"ffn1_l{i}")
        mg = row(p['mix_norm'][i])
        kt, v = _kv(mem, row(p['mem_norm'][i]), p['cross_w_kv'][i])
        cross = (row(p['cross_norm'][i]), p['cross_w_q'][i], kt, v, p['cross_w_o'][i])
        if i == 0:
            ua, sg = _inproj_sgu(x, mg, p['ab_w_in'], row(p['sgu_norm_g']), row(p['sgu_norm_b']),
                                 p['sgu_w_s'], p['sgu_b_s'])
            yf, ybk = _s5_scan(ua, seq_len, p['s5_wb'], p['s5_wq'], p['s5_wt'], p['s5_a_re'], p['s5_a_im'])
            x = _mix_cross(x, seq_len, yf, ybk, ua, sg, row(p['s5_d']), p['s5_w_glu'],
                           p['ab_w_out_a'], p['ab_w_out_b'], *cross)
        else:
            x = _pool_cross(x, seq_len, mg, p['pool_w'], row(p['pool_scale']), *cross)
        x = _ffn(x, row(p['ffn2_norm'][i]), p['ffn2_w_gate'][i], p['ffn2_w_up'][i],
                 p['ffn2_w_down'][i], fg, final_norm=(i == 1), name=f"ffn2_l{i}")
    return x.reshape(n_batch, seq_len, D_MODEL)


def kernel(x_prompt, x_sample, mem_prompt, mem_sample, ffn1_norm, ffn1_w_gate, ffn1_w_up, ffn1_w_down, mix_norm, ab_w_in, s5_lambda_re, s5_lambda_im, s5_log_dt, s5_b_re, s5_b_im, s5_c_re, s5_c_im, s5_d, s5_w_glu, sgu_norm_g, sgu_norm_b, sgu_w_s, sgu_b_s, ab_w_out, pool_w, pool_scale, cross_norm, mem_norm, cross_w_q, cross_w_kv, cross_w_o, ffn2_norm, ffn2_w_gate, ffn2_w_up, ffn2_w_down, final_norm):
    bf = lambda a: a.astype(BF16)
    ar_re, ar_im, wb, wq, wt = _s5_prepare(
        s5_lambda_re[0], s5_lambda_im[0], s5_log_dt[0], s5_b_re[0], s5_b_im[0], s5_c_re[0], s5_c_im[0])
    p = dict(
        ffn1_norm=ffn1_norm, ffn1_w_gate=bf(ffn1_w_gate), ffn1_w_up=bf(ffn1_w_up),
        ffn1_w_down=bf(ffn1_w_down), mix_norm=mix_norm, ab_w_in=bf(ab_w_in[0]),
        s5_wb=wb, s5_wq=wq, s5_wt=wt, s5_a_re=ar_re, s5_a_im=ar_im,
        s5_d=s5_d[0], s5_w_glu=bf(s5_w_glu[0]),
        sgu_norm_g=sgu_norm_g[0], sgu_norm_b=sgu_norm_b[0], sgu_w_s=bf(sgu_w_s[0]),
        sgu_b_s=jnp.broadcast_to(sgu_b_s[0][:, :, None], (SGU_HEADS, SGU_CHUNK, SGU_HEAD_DIM)),
        ab_w_out_a=bf(ab_w_out[0, :S5_WIDTH]), ab_w_out_b=bf(ab_w_out[0, S5_WIDTH:]),
        pool_w=bf(pool_w[0]), pool_scale=pool_scale[0],
        cross_norm=cross_norm, mem_norm=mem_norm, cross_w_q=bf(cross_w_q), cross_w_kv=bf(cross_w_kv),
        cross_w_o=bf(cross_w_o), ffn2_norm=ffn2_norm, ffn2_w_gate=bf(ffn2_w_gate),
        ffn2_w_up=bf(ffn2_w_up), ffn2_w_down=bf(ffn2_w_down), final_norm=final_norm)
    return (_trunk(x_prompt, mem_prompt, p), _trunk(x_sample, mem_sample, p))
```

```python
import functools
import math

import jax
import jax.numpy as jnp
from jax import lax
from jax.experimental import pallas as pl
from jax.experimental.pallas import tpu as pltpu

F32 = jnp.float32
BF16 = jnp.bfloat16

D_MODEL = 1024
D_FF = 4 * D_MODEL
N_BATCH = 8
N_MEM = 256
CROSS_HEADS = 4
CROSS_HEAD_DIM = D_MODEL // CROSS_HEADS
S5_WIDTH = D_MODEL // 2
S5_GROUP = 16
S5_GROUPS = S5_WIDTH // S5_GROUP
S5_STATE = 64
S5_NSTATE = S5_GROUPS * S5_STATE
SGU_WIDTH = D_MODEL // 2
SGU_HEADS = 4
SGU_HEAD_DIM = SGU_WIDTH // SGU_HEADS
SGU_CHUNK = 128
POOL_WINDOWS = (2, 4, 8, 16)
POOL_GROUP_DIM = D_MODEL // len(POOL_WINDOWS)
POOL_HALO = 16
EPS = 1e-6

V7X_LANES = 128
V7X_SUBLANES = 8
V7X_VMEM_BYTES = 64 * 1024 * 1024
VMEM_LIMIT_BYTES = 56 * 1024 * 1024

FFN_ROWS = 512
FFN_FCHUNK = 512
TOK_ROWS = 1024
S5_CHUNK = 128
S5_BLOCK = 4
S5_PERM_STEPS = 32
S5_COLS = 512
S5_BTILE = 256
S5_CTILE = 512
S5_CH_BLOCKS = S5_WIDTH // V7X_LANES


def _cparams(sem):
    return pltpu.CompilerParams(dimension_semantics=sem, vmem_limit_bytes=VMEM_LIMIT_BYTES)


def _resident(shape):
    nd = len(shape)
    return pl.BlockSpec(shape, lambda *_: (0,) * nd, pipeline_mode=pl.Buffered(1))


def _row_spec(tm, width):
    return pl.BlockSpec((tm, width), lambda i: (i, 0))


def _rms(x, g):
    ms = jnp.mean(x * x, axis=-1, keepdims=True)
    return x * lax.rsqrt(ms + EPS) * g


def _gelu(x):
    c = math.sqrt(2.0 / math.pi)
    return x * (0.5 * (1.0 + jnp.tanh(c * (x + 0.044715 * (x * x * x)))))


def _sigmoid(x):
    return 1.0 / (1.0 + jnp.exp(-x))


def _dot(a, b):
    return jnp.dot(a, b, preferred_element_type=F32)


def _ffn_kernel(x_ref, g_ref, wg_ref, wu_ref, wd_ref, fg_ref, o_ref, *, final_norm):
    x = x_ref[...]
    h = _rms(x, g_ref[...]).astype(BF16)
    acc = jnp.zeros(x.shape, F32)
    for c in range(D_FF // FFN_FCHUNK):
        cols = slice(c * FFN_FCHUNK, (c + 1) * FFN_FCHUNK)
        gate = _dot(h, wg_ref[:, cols])
        up = _dot(h, wu_ref[:, cols])
        a = (gate * _sigmoid(gate) * up).astype(BF16)
        acc = acc + _dot(a, wd_ref[cols, :])
    y = x + 0.5 * acc
    if final_norm:
        y = _rms(y, fg_ref[...])
    o_ref[...] = y


def _ffn(x, g, wg, wu, wd, fg, *, final_norm=False, name):
    tm = FFN_ROWS
    n = x.shape[0]
    return pl.pallas_call(
        functools.partial(_ffn_kernel, final_norm=final_norm),
        grid=(n // tm,),
        in_specs=[_row_spec(tm, D_MODEL), _resident((1, D_MODEL)), _resident((D_MODEL, D_FF)),
                  _resident((D_MODEL, D_FF)), _resident((D_FF, D_MODEL)), _resident((1, D_MODEL))],
        out_specs=_row_spec(tm, D_MODEL),
        out_shape=jax.ShapeDtypeStruct((n, D_MODEL), F32),
        compiler_params=_cparams(("arbitrary",)),
        name=name,
    )(x, g, wg, wu, wd, fg)


def _inproj_sgu_kernel(x_ref, g_ref, win_ref, lng_ref, lnb_ref, ws_ref, bs_ref, ua_ref, yb_ref):
    h = _rms(x_ref[...], g_ref[...]).astype(BF16)
    z = _dot(h, win_ref[...])
    ua_ref[...] = z[:, :S5_WIDTH]
    u = _gelu(z[:, S5_WIDTH:S5_WIDTH + SGU_WIDTH])
    v = _gelu(z[:, S5_WIDTH + SGU_WIDTH:])
    mu = jnp.mean(v, axis=-1, keepdims=True)
    vc = v - mu
    var = jnp.mean(vc * vc, axis=-1, keepdims=True)
    v = (vc * lax.rsqrt(var + EPS) * lng_ref[...] + lnb_ref[...]).astype(BF16)
    for n in range(x_ref.shape[0] // SGU_CHUNK):
        rows = slice(n * SGU_CHUNK, (n + 1) * SGU_CHUNK)
        for hd in range(SGU_HEADS):
            cols = slice(hd * SGU_HEAD_DIM, (hd + 1) * SGU_HEAD_DIM)
            vs = _dot(ws_ref[hd], v[rows, cols]) + bs_ref[hd]
            yb_ref[rows, cols] = (u[rows, cols] * vs).astype(BF16)


def _inproj_sgu(x, g, win, lng, lnb, ws, bs):
    tm = TOK_ROWS
    n = x.shape[0]
    return pl.pallas_call(
        _inproj_sgu_kernel,
        grid=(n // tm,),
        in_specs=[_row_spec(tm, D_MODEL), _resident((1, D_MODEL)), _resident(win.shape),
                  _resident((1, SGU_WIDTH)), _resident((1, SGU_WIDTH)),
                  _resident(ws.shape), _resident(bs.shape)],
        out_specs=[_row_spec(tm, S5_WIDTH), _row_spec(tm, SGU_WIDTH)],
        out_shape=[jax.ShapeDtypeStruct((n, S5_WIDTH), F32),
                   jax.ShapeDtypeStruct((n, SGU_WIDTH), BF16)],
        compiler_params=_cparams(("arbitrary",)),
        name="inproj_sgu",
    )(x, g, win, lng, lnb, ws, bs)


def _zoh_a(lam_re, lam_im, log_dt):
    dt = jnp.exp(log_dt)
    mag = jnp.exp(lam_re * dt)
    ang = lam_im * dt
    return mag * jnp.cos(ang), mag * jnp.sin(ang)


def _s5_prep_kernel(lre_ref, lim_ref, ldt_ref, bre_ref, bim_ref, cre_ref, cim_ref,
                    lre3_ref, lim3_ref, ldt3_ref, ctre_ref, ctim_ref,
                    are_ref, aim_ref, wb_ref, wq_ref, wt_ref):
    r, g, grp, lanes = S5_BLOCK, S5_GROUPS, S5_GROUP, V7X_LANES
    lam_re = lre_ref[...]
    lam_im = lim_ref[...]
    a_re, a_im = _zoh_a(lam_re, lam_im, ldt_ref[...])
    nr = a_re - 1.0
    ni = a_im
    den = lam_re * lam_re + lam_im * lam_im
    q_re = ((nr * lam_re + ni * lam_im) / den)[:, None, :]
    q_im = ((ni * lam_re - nr * lam_im) / den)[:, None, :]
    b_re = bre_ref[...]
    b_im = bim_ref[...]
    bb_re = q_re * b_re - q_im * b_im
    bb_im = q_re * b_im + q_im * b_re
    c_re = cre_ref[...]
    c_im = cim_ref[...]
    a3_re, a3_im = _zoh_a(lre3_ref[...], lim3_ref[...], ldt3_ref[...])
    ct_re = ctre_ref[...]
    ct_im = ctim_ref[...]

    bk, ck, kk = [], [], []
    p_re, p_im = jnp.ones_like(a_re), jnp.zeros_like(a_im)
    p3_re, p3_im = jnp.ones_like(a3_re), jnp.zeros_like(a3_im)
    contract_states = (((2,), (2,)), ((0,), (0,)))
    for k in range(r + 1):
        if k < r:
            pr, pi = p_re[:, None, :], p_im[:, None, :]
            bk_re = pr * bb_re - pi * bb_im
            bk_im = pr * bb_im + pi * bb_re
            bk.append((bk_re, bk_im))
            kk.append(lax.dot_general(bk_re, c_re, contract_states, precision=lax.Precision.HIGHEST,
                                      preferred_element_type=F32)
                      - lax.dot_general(bk_im, c_im, contract_states, precision=lax.Precision.HIGHEST,
                                        preferred_element_type=F32))
        if k >= 1:
            ck.append((p3_re * ct_re - p3_im * ct_im, -(p3_re * ct_im + p3_im * ct_re)))
        if k == r:
            are_ref[...] = p_re
            aim_ref[...] = p_im
        p_re, p_im = p_re * a_re - p_im * a_im, p_re * a_im + p_im * a_re
        p3_re, p3_im = p3_re * a3_re - p3_im * a3_im, p3_re * a3_im + p3_im * a3_re

    wb_ref[...] = jnp.zeros(wb_ref.shape, BF16)
    wq_ref[...] = jnp.zeros(wq_ref.shape, BF16)
    wt_ref[...] = jnp.zeros(wt_ref.shape, BF16)
    groups_per_btile = S5_BTILE // S5_STATE
    groups_per_lanes = lanes // grp
    for d in range(2):
        k_in = [r - 1 - s if d == 0 else s for s in range(r)]
        k_out = [s + 1 if d == 0 else r - s for s in range(r)]
        for j in range(S5_NSTATE // S5_BTILE):
            for gt in range(groups_per_btile):
                gi = j * groups_per_btile + gt
                gl = gi % groups_per_lanes
                for s in range(r):
                    rows = slice(s * lanes + gl * grp, s * lanes + (gl + 1) * grp)
                    cols = slice(gt * S5_STATE, (gt + 1) * S5_STATE)
                    for part in range(2):
                        wb_ref[d, part, j, rows, cols] = bk[k_in[s]][part][d * g + gi].astype(BF16)
        for m in range(S5_CH_BLOCKS):
            for gl in range(groups_per_lanes):
                gi = m * groups_per_lanes + gl
                srows = slice(gl * S5_STATE, (gl + 1) * S5_STATE)
                for s in range(r):
                    cols = slice(s * lanes + gl * grp, s * lanes + (gl + 1) * grp)
                    for part in range(2):
                        wq_ref[d, part, m, srows, cols] = ck[k_out[s] - 1][part][d * g + gi].astype(BF16)
                    for s_out in range(r):
                        lag = (s_out - s) if d == 0 else (s - s_out)
                        if lag >= 0:
                            ocols = slice(s_out * lanes + gl * grp, s_out * lanes + (gl + 1) * grp)
                            wt_ref[d, m, cols, ocols] = kk[lag][d * g + gi].astype(BF16)


def _s5_prepare(lam_re, lam_im, log_dt, b_re, b_im, c_re, c_im):
    g2 = 2 * S5_GROUPS
    r = S5_BLOCK
    rl = r * V7X_LANES
    flat = lambda a: a.reshape(g2, S5_STATE)
    gcp = lambda a: a.reshape(g2, S5_GROUP, S5_STATE)
    gpc = lambda a: jnp.swapaxes(gcp(a), -1, -2)
    on_sublanes = lambda a: jnp.broadcast_to(a.reshape(g2, -1, 1), (g2, S5_STATE, S5_GROUP))
    a_re, a_im, wb, wq, wt = pl.pallas_call(
        _s5_prep_kernel,
        out_shape=[jax.ShapeDtypeStruct((g2, S5_STATE), F32)] * 2
        + [jax.ShapeDtypeStruct((2, 2, S5_NSTATE // S5_BTILE, rl, S5_BTILE), BF16),
           jax.ShapeDtypeStruct((2, 2, S5_CH_BLOCKS, S5_CTILE, rl), BF16),
           jax.ShapeDtypeStruct((2, S5_CH_BLOCKS, rl, rl), BF16)],
        compiler_params=pltpu.CompilerParams(vmem_limit_bytes=VMEM_LIMIT_BYTES),
        name="s5_prepare",
    )(flat(lam_re), flat(lam_im), log_dt.reshape(g2, 1),
      gcp(jnp.swapaxes(b_re, -1, -2)), gcp(jnp.swapaxes(b_im, -1, -2)), gcp(c_re), gcp(c_im),
      on_sublanes(lam_re), on_sublanes(lam_im), on_sublanes(log_dt), gpc(c_re), gpc(c_im))
    return a_re.reshape(2, S5_NSTATE), a_im.reshape(2, S5_NSTATE), wb, wq, wt


def _s5_scan_kernel(uf_ref, ub_ref, perm_ref, permt_ref, wb_ref, wq_ref, wt_ref, are_ref, aim_ref,
                    yf_ref, yb_ref, utb_ref, up_ref, sre_ref, sim_ref, ytb_ref, st_ref):
    lc, r, nb, lanes = S5_CHUNK, S5_BLOCK, N_BATCH, V7X_LANES
    nblk = lc // r
    brows = nblk * nb
    prow = S5_PERM_STEPS * nb

    @pl.when(pl.program_id(0) == 0)
    def _():
        st_ref[...] = jnp.zeros(st_ref.shape, F32)

    for d, (u_ref, y_ref) in enumerate(((uf_ref, yf_ref), (ub_ref, yb_ref))):
        for h in range(lc // S5_PERM_STEPS):
            blk = u_ref[:, h * S5_PERM_STEPS:(h + 1) * S5_PERM_STEPS, :].reshape(prow, S5_WIDTH)
            utb_ref[h * prow:(h + 1) * prow, :] = _dot(perm_ref[...], blk.astype(BF16))
        for j in range(nblk):
            for s in range(r):
                slab = utb_ref[(j * r + s) * nb:(j * r + s + 1) * nb, :]
                for kb in range(S5_CH_BLOCKS):
                    up_ref[j * nb:(j + 1) * nb, (kb * r + s) * lanes:(kb * r + s + 1) * lanes] = (
                        slab[:, kb * lanes:(kb + 1) * lanes])

        off = nb if d == 0 else 0
        carry = 0 if d == 0 else brows
        ch_per_btile = S5_BTILE // S5_STATE * S5_GROUP
        for j in range(S5_NSTATE // S5_BTILE):
            kb = (j * ch_per_btile) // lanes
            cols = slice(j * S5_BTILE, (j + 1) * S5_BTILE)
            lhs = up_ref[:, kb * r * lanes:(kb + 1) * r * lanes].astype(BF16)
            sre_ref[off:off + brows, cols] = _dot(lhs, wb_ref[d, 0, j])
            sim_ref[off:off + brows, cols] = _dot(lhs, wb_ref[d, 1, j])

        for cb in range(S5_NSTATE // S5_COLS):
            cols = slice(cb * S5_COLS, (cb + 1) * S5_COLS)
            a_re = jnp.broadcast_to(are_ref[d:d + 1, cols], (nb, S5_COLS))
            a_im = jnp.broadcast_to(aim_ref[d:d + 1, cols], (nb, S5_COLS))
            h_re = st_ref[d, 0, :, cols]
            h_im = st_ref[d, 1, :, cols]
            sre_ref[carry:carry + nb, cols] = h_re
            sim_ref[carry:carry + nb, cols] = h_im
            for jj in range(nblk):
                j = jj if d == 0 else nblk - 1 - jj
                rows = slice(off + j * nb, off + (j + 1) * nb)
                h_re, h_im = (a_re * h_re - a_im * h_im + sre_ref[rows, cols],
                              a_re * h_im + a_im * h_re + sim_ref[rows, cols])
                sre_ref[rows, cols] = h_re
                sim_ref[rows, cols] = h_im
            st_ref[d, 0, :, cols] = h_re
            st_ref[d, 1, :, cols] = h_im

        enter = slice(0, brows) if d == 0 else slice(nb, brows + nb)
        for m in range(S5_CH_BLOCKS):
            srows = slice(m * S5_CTILE, (m + 1) * S5_CTILE)
            ym = (_dot(sre_ref[enter, srows].astype(BF16), wq_ref[d, 0, m])
                  + _dot(sim_ref[enter, srows].astype(BF16), wq_ref[d, 1, m])
                  + _dot(up_ref[:, m * r * lanes:(m + 1) * r * lanes].astype(BF16), wt_ref[d, m]))
            for j in range(nblk):
                for s in range(r):
                    ytb_ref[(j * r + s) * nb:(j * r + s + 1) * nb, m * lanes:(m + 1) * lanes] = (
                        ym[j * nb:(j + 1) * nb, s * lanes:(s + 1) * lanes])

        for h in range(lc // S5_PERM_STEPS):
            y = ytb_ref[h * prow:(h + 1) * prow, :]
            y_hi = y.astype(BF16)
            y_lo = (y - y_hi.astype(F32)).astype(BF16)
            y = _dot(permt_ref[...], y_hi) + _dot(permt_ref[...], y_lo)
            y_ref[:, h * S5_PERM_STEPS:(h + 1) * S5_PERM_STEPS, :] = y.reshape(nb, S5_PERM_STEPS, S5_WIDTH)


def _time_major_permutation():
    n = S5_PERM_STEPS * N_BATCH
    r = jnp.arange(n)
    src = (r % N_BATCH) * S5_PERM_STEPS + r // N_BATCH
    return jax.nn.one_hot(src, n, dtype=BF16)


def _s5_scan(ua, seq_len, wb, wq, wt, a_re, a_im):
    nc = seq_len // S5_CHUNK
    rows = S5_CHUNK * N_BATCH
    brows = rows // S5_BLOCK
    ua = ua.reshape(N_BATCH, seq_len, S5_WIDTH)
    perm = _time_major_permutation()
    fwd = pl.BlockSpec((N_BATCH, S5_CHUNK, S5_WIDTH), lambda c: (0, c, 0))
    bwd = pl.BlockSpec((N_BATCH, S5_CHUNK, S5_WIDTH), lambda c: (0, nc - 1 - c, 0))
    yf, yb = pl.pallas_call(
        _s5_scan_kernel,
        grid=(nc,),
        in_specs=[fwd, bwd, _resident(perm.shape), _resident(perm.shape), _resident(wb.shape),
                  _resident(wq.shape), _resident(wt.shape), _resident(a_re.shape), _resident(a_im.shape)],
        out_specs=[fwd, bwd],
        out_shape=[jax.ShapeDtypeStruct((N_BATCH, seq_len, S5_WIDTH), F32)] * 2,
        scratch_shapes=[pltpu.VMEM((rows, S5_WIDTH), F32),
                        pltpu.VMEM((brows, S5_BLOCK * S5_WIDTH), F32),
                        pltpu.VMEM((brows + N_BATCH, S5_NSTATE), F32),
                        pltpu.VMEM((brows + N_BATCH, S5_NSTATE), F32),
                        pltpu.VMEM((rows, S5_WIDTH), F32),
                        pltpu.VMEM((2, 2, N_BATCH, S5_NSTATE), F32)],
        compiler_params=_cparams(("arbitrary",)),
        name="s5_scan",
    )(ua, ua, perm, perm.T, wb, wq, wt, a_re, a_im)
    n = N_BATCH * seq_len
    return yf.reshape(n, S5_WIDTH), yb.reshape(n, S5_WIDTH)


def _mix_out(x, yf_ref, ybk_ref, ua_ref, sg_ref, d_ref, wglu_ref, woa_ref, wob_ref):
    y = yf_ref[...] + ybk_ref[...] + d_ref[...] * ua_ref[...]
    g = _gelu(y)
    ya = g * _sigmoid(_dot(g.astype(BF16), wglu_ref[...]))
    return x + _dot(ya.astype(BF16), woa_ref[...]) + _dot(sg_ref[...], wob_ref[...])


def _pool(x, xp_ref, xn_ref, g_ref, pw_ref, ps_ref, hp_ref, sa_ref, sb_ref, *, seq_len):
    tm = x.shape[0]
    halo = POOL_HALO
    tiles_per_seq = seq_len // tm
    it = pl.program_id(0) % tiles_per_seq
    g = g_ref[...]
    h = _rms(x, g)
    keep_prev = (it > 0).astype(F32)
    keep_next = (it < tiles_per_seq - 1).astype(F32)
    hp_ref[0:halo, :] = _rms(xp_ref[...], g) * keep_prev
    hp_ref[halo:halo + tm, :] = h
    hp_ref[halo + tm:, :] = _rms(xn_ref[...], g) * keep_next
    lo, n = halo // 2, tm + halo
    tail = slice(lo + n, tm + 2 * halo)
    sa_ref[tail, :] = jnp.zeros((halo // 2, D_MODEL), F32)
    sb_ref[tail, :] = jnp.zeros((halo // 2, D_MODEL), F32)

    def doubled(src_ref, dst_ref, step, cols):
        dst_ref[lo:lo + n, cols] = src_ref[lo:lo + n, cols] + src_ref[lo + step:lo + step + n, cols]

    def window(ref, first, cols):
        return ref[first:first + tm, cols] + ref[halo:halo + tm, cols]

    t = it * tm + lax.broadcasted_iota(jnp.int32, (tm, 1), 0)
    outs = []
    for gi, w in enumerate(POOL_WINDOWS):
        cols = slice(gi * POOL_GROUP_DIM, (gi + 1) * POOL_GROUP_DIM)
        if w == 2:
            s = window(hp_ref, halo - 1, cols)
        elif w == 4:
            doubled(hp_ref, sa_ref, 1, cols)
            s = window(sa_ref, halo - 2, cols)
        elif w == 8:
            doubled(hp_ref, sa_ref, 1, cols)
            doubled(sa_ref, sb_ref, 2, cols)
            s = window(sb_ref, halo - 4, cols)
        else:
            assert w == 16
            doubled(hp_ref, sa_ref, 1, cols)
            doubled(sa_ref, sb_ref, 2, cols)
            doubled(sb_ref, sa_ref, 4, cols)
            s = window(sa_ref, halo - 8, cols)
        lo_t = jnp.clip(t - w // 2, 0, seq_len)
        hi_t = jnp.clip(t + w // 2, 0, seq_len)
        inv_cnt = 1.0 / (hi_t - lo_t).astype(F32)
        pooled = s * inv_cnt - h[:, cols]
        outs.append(_dot(pooled.astype(BF16), pw_ref[gi]))
    return x + jnp.concatenate(outs, axis=-1) * ps_ref[...]


def _kv_kernel(m_ref, g_ref, wkv_ref, kt_ref, v_ref):
    hm = _rms(m_ref[...], g_ref[...]).astype(BF16)
    kv = _dot(hm, wkv_ref[...])
    kt_ref[...] = kv[:, :D_MODEL].T.astype(BF16)
    v_ref[...] = kv[:, D_MODEL:].astype(BF16)


def _kv(mem, g, wkv):
    return pl.pallas_call(
        _kv_kernel,
        grid=(N_BATCH,),
        in_specs=[pl.BlockSpec((None, N_MEM, D_MODEL), lambda b: (b, 0, 0)),
                  _resident((1, D_MODEL)), _resident(wkv.shape)],
        out_specs=[pl.BlockSpec((None, D_MODEL, N_MEM), lambda b: (b, 0, 0)),
                   pl.BlockSpec((None, N_MEM, D_MODEL), lambda b: (b, 0, 0))],
        out_shape=[jax.ShapeDtypeStruct((N_BATCH, D_MODEL, N_MEM), BF16),
                   jax.ShapeDtypeStruct((N_BATCH, N_MEM, D_MODEL), BF16)],
        compiler_params=_cparams(("arbitrary",)),
        name="mem_kv",
    )(mem, g, wkv)


def _cross_attn(x, g_ref, wq_ref, kt_ref, v_ref, wo_ref):
    h = _rms(x, g_ref[...]).astype(BF16)
    q = (_dot(h, wq_ref[...]) * (CROSS_HEAD_DIM ** -0.5)).astype(BF16)
    heads = []
    for hd in range(CROSS_HEADS):
        cols = slice(hd * CROSS_HEAD_DIM, (hd + 1) * CROSS_HEAD_DIM)
        s = _dot(q[:, cols], kt_ref[cols, :])
        e = jnp.exp(s - jnp.max(s, axis=-1, keepdims=True))
        p = e * (1.0 / jnp.sum(e, axis=-1, keepdims=True))
        heads.append(_dot(p.astype(BF16), v_ref[:, cols]).astype(BF16))
    o = jnp.concatenate(heads, axis=-1)
    return x + _dot(o, wo_ref[...])


def _mix_cross_kernel(x_ref, yf_ref, ybk_ref, ua_ref, sg_ref, d_ref, wglu_ref, woa_ref, wob_ref,
                      g_ref, wq_ref, kt_ref, v_ref, wo_ref, o_ref):
    x = _mix_out(x_ref[...], yf_ref, ybk_ref, ua_ref, sg_ref, d_ref, wglu_ref, woa_ref, wob_ref)
    o_ref[...] = _cross_attn(x, g_ref, wq_ref, kt_ref, v_ref, wo_ref)


def _pool_cross_kernel(x_ref, xp_ref, xn_ref, mg_ref, pw_ref, ps_ref,
                       g_ref, wq_ref, kt_ref, v_ref, wo_ref, o_ref, hp_ref, sa_ref, sb_ref, *, seq_len):
    x = _pool(x_ref[...], xp_ref, xn_ref, mg_ref, pw_ref, ps_ref, hp_ref, sa_ref, sb_ref, seq_len=seq_len)
    o_ref[...] = _cross_attn(x, g_ref, wq_ref, kt_ref, v_ref, wo_ref)


def _cross_specs(seq_len, tm, wq, wo):
    tiles_per_seq = seq_len // tm
    return [_resident((1, D_MODEL)), _resident(wq.shape),
            pl.BlockSpec((None, D_MODEL, N_MEM), lambda i: (i // tiles_per_seq, 0, 0)),
            pl.BlockSpec((None, N_MEM, D_MODEL), lambda i: (i // tiles_per_seq, 0, 0)),
            _resident(wo.shape)]


def _mix_cross(x, seq_len, yf, ybk, ua, sg, d, wglu, woa, wob, g, wq, kt, v, wo):
    tm = TOK_ROWS
    n = x.shape[0]
    return pl.pallas_call(
        _mix_cross_kernel,
        grid=(n // tm,),
        in_specs=[_row_spec(tm, D_MODEL), _row_spec(tm, S5_WIDTH), _row_spec(tm, S5_WIDTH),
                  _row_spec(tm, S5_WIDTH), _row_spec(tm, SGU_WIDTH),
                  _resident((1, S5_WIDTH)), _resident(wglu.shape), _resident(woa.shape),
                  _resident(wob.shape)] + _cross_specs(seq_len, tm, wq, wo),
        out_specs=_row_spec(tm, D_MODEL),
        out_shape=jax.ShapeDtypeStruct((n, D_MODEL), F32),
        compiler_params=_cparams(("arbitrary",)),
        name="mix_cross",
    )(x, yf, ybk, ua, sg, d, wglu, woa, wob, g, wq, kt, v, wo)


def _pool_cross(x, seq_len, mg, pw, ps, g, wq, kt, v, wo):
    tm = TOK_ROWS
    n = x.shape[0]
    per = tm // POOL_HALO
    last = n // POOL_HALO - 1
    prev = pl.BlockSpec((POOL_HALO, D_MODEL), lambda i: (jnp.maximum(i * per - 1, 0), 0))
    nxt = pl.BlockSpec((POOL_HALO, D_MODEL), lambda i: (jnp.minimum((i + 1) * per, last), 0))
    padded = pltpu.VMEM((tm + 2 * POOL_HALO, D_MODEL), F32)
    return pl.pallas_call(
        functools.partial(_pool_cross_kernel, seq_len=seq_len),
        grid=(n // tm,),
        in_specs=[_row_spec(tm, D_MODEL), prev, nxt, _resident((1, D_MODEL)), _resident(pw.shape),
                  _resident((1, D_MODEL))] + _cross_specs(seq_len, tm, wq, wo),
        out_specs=_row_spec(tm, D_MODEL),
        out_shape=jax.ShapeDtypeStruct((n, D_MODEL), F32),
        scratch_shapes=[padded, padded, padded],
        compiler_params=_cparams(("arbitrary",)),
        name="pool_cross",
    )(x, x, x, mg, pw, ps, g, wq, kt, v, wo)


def _trunk(x, mem, p):
    n_batch, seq_len, _ = x.shape
    assert n_batch == N_BATCH and seq_len % max(FFN_ROWS, TOK_ROWS) == 0 and seq_len % S5_CHUNK == 0
    x = x.reshape(n_batch * seq_len, D_MODEL)
    row = lambda a: a.reshape(1, -1)
    fg = row(p['final_norm'])
    for i in range(2):
        x = _ffn(x, row(p['ffn1_norm'][i]), p['ffn1_w_gate'][i], p['ffn1_w_up'][i],
                 p['ffn1_w_down'][i], fg, name=f"ffn1_l{i}")
        mg = row(p['mix_norm'][i])
        kt, v = _kv(mem, row(p['mem_norm'][i]), p['cross_w_kv'][i])
        cross = (row(p['cross_norm'][i]), p['cross_w_q'][i], kt, v, p['cross_w_o'][i])
        if i == 0:
            ua, sg = _inproj_sgu(x, mg, p['ab_w_in'], row(p['sgu_norm_g']), row(p['sgu_norm_b']),
                                 p['sgu_w_s'], p['sgu_b_s'])
            yf, ybk = _s5_scan(ua, seq_len, p['s5_wb'], p['s5_wq'], p['s5_wt'], p['s5_a_re'], p['s5_a_im'])
            x = _mix_cross(x, seq_len, yf, ybk, ua, sg, row(p['s5_d']), p['s5_w_glu'],
                           p['ab_w_out_a'], p['ab_w_out_b'], *cross)
        else:
            x = _pool_cross(x, seq_len, mg, p['pool_w'], row(p['pool_scale']), *cross)
        x = _ffn(x, row(p['ffn2_norm'][i]), p['ffn2_w_gate'][i], p['ffn2_w_up'][i],
                 p['ffn2_w_down'][i], fg, final_norm=(i == 1), name=f"ffn2_l{i}")
    return x.reshape(n_batch, seq_len, D_MODEL)


def kernel(x_prompt, x_sample, mem_prompt, mem_sample, ffn1_norm, ffn1_w_gate, ffn1_w_up, ffn1_w_down, mix_norm, ab_w_in, s5_lambda_re, s5_lambda_im, s5_log_dt, s5_b_re, s5_b_im, s5_c_re, s5_c_im, s5_d, s5_w_glu, sgu_norm_g, sgu_norm_b, sgu_w_s, sgu_b_s, ab_w_out, pool_w, pool_scale, cross_norm, mem_norm, cross_w_q, cross_w_kv, cross_w_o, ffn2_norm, ffn2_w_gate, ffn2_w_up, ffn2_w_down, final_norm):
    bf = lambda a: a.astype(BF16)
    per_layer = lambda a: [bf(a[i]) for i in range(a.shape[0])]
    ar_re, ar_im, wb, wq, wt = _s5_prepare(
        s5_lambda_re[0], s5_lambda_im[0], s5_log_dt[0], s5_b_re[0], s5_b_im[0], s5_c_re[0], s5_c_im[0])
    p = dict(
        ffn1_norm=ffn1_norm, ffn1_w_gate=per_layer(ffn1_w_gate), ffn1_w_up=per_layer(ffn1_w_up),
        ffn1_w_down=per_layer(ffn1_w_down), mix_norm=mix_norm, ab_w_in=bf(ab_w_in[0]),
        s5_wb=wb, s5_wq=wq, s5_wt=wt, s5_a_re=ar_re, s5_a_im=ar_im,
        s5_d=s5_d[0], s5_w_glu=bf(s5_w_glu[0]),
        sgu_norm_g=sgu_norm_g[0], sgu_norm_b=sgu_norm_b[0], sgu_w_s=bf(sgu_w_s[0]),
        sgu_b_s=jnp.broadcast_to(sgu_b_s[0][:, :, None], (SGU_HEADS, SGU_CHUNK, SGU_HEAD_DIM)),
        ab_w_out_a=bf(ab_w_out[0, :S5_WIDTH]), ab_w_out_b=bf(ab_w_out[0, S5_WIDTH:]),
        pool_w=bf(pool_w[0]), pool_scale=pool_scale[0],
        cross_norm=cross_norm, mem_norm=mem_norm, cross_w_q=per_layer(cross_w_q),
        cross_w_kv=per_layer(cross_w_kv), cross_w_o=per_layer(cross_w_o), ffn2_norm=ffn2_norm,
        ffn2_w_gate=per_layer(ffn2_w_gate), ffn2_w_up=per_layer(ffn2_w_up),
        ffn2_w_down=per_layer(ffn2_w_down), final_norm=final_norm)
    return (_trunk(x_prompt, mem_prompt, p), _trunk(x_sample, mem_sample, p))
```

```python
import functools
import math

import jax
import jax.numpy as jnp
from jax import lax
from jax.experimental import pallas as pl
from jax.experimental.pallas import tpu as pltpu

F32 = jnp.float32
BF16 = jnp.bfloat16

D_MODEL = 1024
D_FF = 4 * D_MODEL
N_BATCH = 8
N_MEM = 256
CROSS_HEADS = 4
CROSS_HEAD_DIM = D_MODEL // CROSS_HEADS
S5_WIDTH = D_MODEL // 2
S5_GROUP = 16
S5_GROUPS = S5_WIDTH // S5_GROUP
S5_STATE = 64
S5_NSTATE = S5_GROUPS * S5_STATE
SGU_WIDTH = D_MODEL // 2
SGU_HEADS = 4
SGU_HEAD_DIM = SGU_WIDTH // SGU_HEADS
SGU_CHUNK = 128
POOL_WINDOWS = (2, 4, 8, 16)
POOL_GROUP_DIM = D_MODEL // len(POOL_WINDOWS)
POOL_HALO = 16
EPS = 1e-6

V7X_LANES = 128
V7X_SUBLANES = 8
V7X_VMEM_BYTES = 64 * 1024 * 1024
VMEM_LIMIT_BYTES = 56 * 1024 * 1024

FFN_ROWS = 512
FFN_FCHUNK = 512
TOK_ROWS = 1024
S5_CHUNK = 128
S5_BLOCK = 4
S5_PERM_STEPS = 32
S5_COLS = 512
S5_BTILE = 256
S5_CTILE = 512
S5_CH_BLOCKS = S5_WIDTH // V7X_LANES


def _cparams(sem):
    return pltpu.CompilerParams(dimension_semantics=sem, vmem_limit_bytes=VMEM_LIMIT_BYTES)


def _resident(shape):
    nd = len(shape)
    return pl.BlockSpec(shape, lambda *_: (0,) * nd, pipeline_mode=pl.Buffered(1))


def _resident_layer(stacked_shape, layer):
    _, rows, cols = stacked_shape
    return pl.BlockSpec((None, rows, cols), lambda *_: (layer, 0, 0), pipeline_mode=pl.Buffered(1))


def _row_spec(tm, width):
    return pl.BlockSpec((tm, width), lambda i: (i, 0))


def _rms(x, g):
    ms = jnp.mean(x * x, axis=-1, keepdims=True)
    return x * lax.rsqrt(ms + EPS) * g


def _gelu(x):
    c = math.sqrt(2.0 / math.pi)
    return x * (0.5 * (1.0 + jnp.tanh(c * (x + 0.044715 * (x * x * x)))))


def _sigmoid(x):
    return 1.0 / (1.0 + jnp.exp(-x))


def _dot(a, b):
    return jnp.dot(a, b, preferred_element_type=F32)


def _ffn_kernel(x_ref, g_ref, wg_ref, wu_ref, wd_ref, fg_ref, o_ref, *, final_norm):
    x = x_ref[...]
    h = _rms(x, g_ref[...]).astype(BF16)
    acc = jnp.zeros(x.shape, F32)
    for c in range(D_FF // FFN_FCHUNK):
        cols = slice(c * FFN_FCHUNK, (c + 1) * FFN_FCHUNK)
        gate = _dot(h, wg_ref[:, cols])
        up = _dot(h, wu_ref[:, cols])
        a = (gate * _sigmoid(gate) * up).astype(BF16)
        acc = acc + _dot(a, wd_ref[cols, :])
    y = x + 0.5 * acc
    if final_norm:
        y = _rms(y, fg_ref[...])
    o_ref[...] = y


def _ffn(x, layer, g, wg, wu, wd, fg, *, final_norm=False, name):
    tm = FFN_ROWS
    n = x.shape[0]
    return pl.pallas_call(
        functools.partial(_ffn_kernel, final_norm=final_norm),
        grid=(n // tm,),
        in_specs=[_row_spec(tm, D_MODEL), _resident((1, D_MODEL)), _resident_layer(wg.shape, layer),
                  _resident_layer(wu.shape, layer), _resident_layer(wd.shape, layer),
                  _resident((1, D_MODEL))],
        out_specs=_row_spec(tm, D_MODEL),
        out_shape=jax.ShapeDtypeStruct((n, D_MODEL), F32),
        compiler_params=_cparams(("arbitrary",)),
        name=name,
    )(x, g, wg, wu, wd, fg)


def _inproj_sgu_kernel(x_ref, g_ref, win_ref, lng_ref, lnb_ref, ws_ref, bs_ref, ua_ref, yb_ref):
    h = _rms(x_ref[...], g_ref[...]).astype(BF16)
    z = _dot(h, win_ref[...])
    ua_ref[...] = z[:, :S5_WIDTH]
    u = _gelu(z[:, S5_WIDTH:S5_WIDTH + SGU_WIDTH])
    v = _gelu(z[:, S5_WIDTH + SGU_WIDTH:])
    mu = jnp.mean(v, axis=-1, keepdims=True)
    vc = v - mu
    var = jnp.mean(vc * vc, axis=-1, keepdims=True)
    v = (vc * lax.rsqrt(var + EPS) * lng_ref[...] + lnb_ref[...]).astype(BF16)
    for n in range(x_ref.shape[0] // SGU_CHUNK):
        rows = slice(n * SGU_CHUNK, (n + 1) * SGU_CHUNK)
        for hd in range(SGU_HEADS):
            cols = slice(hd * SGU_HEAD_DIM, (hd + 1) * SGU_HEAD_DIM)
            vs = _dot(ws_ref[hd], v[rows, cols]) + bs_ref[hd]
            yb_ref[rows, cols] = (u[rows, cols] * vs).astype(BF16)


def _inproj_sgu(x, g, win, lng, lnb, ws, bs):
    tm = TOK_ROWS
    n = x.shape[0]
    return pl.pallas_call(
        _inproj_sgu_kernel,
        grid=(n // tm,),
        in_specs=[_row_spec(tm, D_MODEL), _resident((1, D_MODEL)), _resident(win.shape),
                  _resident((1, SGU_WIDTH)), _resident((1, SGU_WIDTH)),
                  _resident(ws.shape), _resident(bs.shape)],
        out_specs=[_row_spec(tm, S5_WIDTH), _row_spec(tm, SGU_WIDTH)],
        out_shape=[jax.ShapeDtypeStruct((n, S5_WIDTH), F32),
                   jax.ShapeDtypeStruct((n, SGU_WIDTH), BF16)],
        compiler_params=_cparams(("arbitrary",)),
        name="inproj_sgu",
    )(x, g, win, lng, lnb, ws, bs)


def _zoh_a(lam_re, lam_im, log_dt):
    dt = jnp.exp(log_dt)
    mag = jnp.exp(lam_re * dt)
    ang = lam_im * dt
    return mag * jnp.cos(ang), mag * jnp.sin(ang)


def _s5_prep_kernel(lre_ref, lim_ref, ldt_ref, bre_ref, bim_ref, cre_ref, cim_ref,
                    lre3_ref, lim3_ref, ldt3_ref, ctre_ref, ctim_ref,
                    are_ref, aim_ref, wb_ref, wq_ref, wt_ref):
    r, g, grp, lanes = S5_BLOCK, S5_GROUPS, S5_GROUP, V7X_LANES
    lam_re = lre_ref[...]
    lam_im = lim_ref[...]
    a_re, a_im = _zoh_a(lam_re, lam_im, ldt_ref[...])
    nr = a_re - 1.0
    ni = a_im
    den = lam_re * lam_re + lam_im * lam_im
    q_re = ((nr * lam_re + ni * lam_im) / den)[:, None, :]
    q_im = ((ni * lam_re - nr * lam_im) / den)[:, None, :]
    b_re = bre_ref[...]
    b_im = bim_ref[...]
    bb_re = q_re * b_re - q_im * b_im
    bb_im = q_re * b_im + q_im * b_re
    c_re = cre_ref[...]
    c_im = cim_ref[...]
    a3_re, a3_im = _zoh_a(lre3_ref[...], lim3_ref[...], ldt3_ref[...])
    ct_re = ctre_ref[...]
    ct_im = ctim_ref[...]

    bk, ck, kk = [], [], []
    p_re, p_im = jnp.ones_like(a_re), jnp.zeros_like(a_im)
    p3_re, p3_im = jnp.ones_like(a3_re), jnp.zeros_like(a3_im)
    contract_states = (((2,), (2,)), ((0,), (0,)))
    for k in range(r + 1):
        if k < r:
            pr, pi = p_re[:, None, :], p_im[:, None, :]
            bk_re = pr * bb_re - pi * bb_im
            bk_im = pr * bb_im + pi * bb_re
            bk.append((bk_re, bk_im))
            kk.append(lax.dot_general(bk_re, c_re, contract_states, precision=lax.Precision.HIGHEST,
                                      preferred_element_type=F32)
                      - lax.dot_general(bk_im, c_im, contract_states, precision=lax.Precision.HIGHEST,
                                        preferred_element_type=F32))
        if k >= 1:
            ck.append((p3_re * ct_re - p3_im * ct_im, -(p3_re * ct_im + p3_im * ct_re)))
        if k == r:
            are_ref[...] = p_re
            aim_ref[...] = p_im
        p_re, p_im = p_re * a_re - p_im * a_im, p_re * a_im + p_im * a_re
        p3_re, p3_im = p3_re * a3_re - p3_im * a3_im, p3_re * a3_im + p3_im * a3_re

    wb_ref[...] = jnp.zeros(wb_ref.shape, BF16)
    wq_ref[...] = jnp.zeros(wq_ref.shape, BF16)
    wt_ref[...] = jnp.zeros(wt_ref.shape, BF16)
    groups_per_btile = S5_BTILE // S5_STATE
    groups_per_lanes = lanes // grp
    for d in range(2):
        k_in = [r - 1 - s if d == 0 else s for s in range(r)]
        k_out = [s + 1 if d == 0 else r - s for s in range(r)]
        for j in range(S5_NSTATE // S5_BTILE):
            for gt in range(groups_per_btile):
                gi = j * groups_per_btile + gt
                gl = gi % groups_per_lanes
                for s in range(r):
                    rows = slice(s * lanes + gl * grp, s * lanes + (gl + 1) * grp)
                    cols = slice(gt * S5_STATE, (gt + 1) * S5_STATE)
                    for part in range(2):
                        wb_ref[d, part, j, rows, cols] = bk[k_in[s]][part][d * g + gi].astype(BF16)
        for m in range(S5_CH_BLOCKS):
            for gl in range(groups_per_lanes):
                gi = m * groups_per_lanes + gl
                srows = slice(gl * S5_STATE, (gl + 1) * S5_STATE)
                for s in range(r):
                    cols = slice(s * lanes + gl * grp, s * lanes + (gl + 1) * grp)
                    for part in range(2):
                        wq_ref[d, part, m, srows, cols] = ck[k_out[s] - 1][part][d * g + gi].astype(BF16)
                    for s_out in range(r):
                        lag = (s_out - s) if d == 0 else (s - s_out)
                        if lag >= 0:
                            ocols = slice(s_out * lanes + gl * grp, s_out * lanes + (gl + 1) * grp)
                            wt_ref[d, m, cols, ocols] = kk[lag][d * g + gi].astype(BF16)


def _s5_prepare(lam_re, lam_im, log_dt, b_re, b_im, c_re, c_im):
    g2 = 2 * S5_GROUPS
    r = S5_BLOCK
    rl = r * V7X_LANES
    flat = lambda a: a.reshape(g2, S5_STATE)
    gcp = lambda a: a.reshape(g2, S5_GROUP, S5_STATE)
    gpc = lambda a: jnp.swapaxes(gcp(a), -1, -2)
    on_sublanes = lambda a: jnp.broadcast_to(a.reshape(g2, -1, 1), (g2, S5_STATE, S5_GROUP))
    a_re, a_im, wb, wq, wt = pl.pallas_call(
        _s5_prep_kernel,
        out_shape=[jax.ShapeDtypeStruct((g2, S5_STATE), F32)] * 2
        + [jax.ShapeDtypeStruct((2, 2, S5_NSTATE // S5_BTILE, rl, S5_BTILE), BF16),
           jax.ShapeDtypeStruct((2, 2, S5_CH_BLOCKS, S5_CTILE, rl), BF16),
           jax.ShapeDtypeStruct((2, S5_CH_BLOCKS, rl, rl), BF16)],
        compiler_params=pltpu.CompilerParams(vmem_limit_bytes=VMEM_LIMIT_BYTES),
        name="s5_prepare",
    )(flat(lam_re), flat(lam_im), log_dt.reshape(g2, 1),
      gcp(jnp.swapaxes(b_re, -1, -2)), gcp(jnp.swapaxes(b_im, -1, -2)), gcp(c_re), gcp(c_im),
      on_sublanes(lam_re), on_sublanes(lam_im), on_sublanes(log_dt), gpc(c_re), gpc(c_im))
    return a_re.reshape(2, S5_NSTATE), a_im.reshape(2, S5_NSTATE), wb, wq, wt


def _s5_scan_kernel(uf_ref, ub_ref, perm_ref, permt_ref, wb_ref, wq_ref, wt_ref, are_ref, aim_ref,
                    yf_ref, yb_ref, utb_ref, up_ref, sre_ref, sim_ref, ytb_ref, st_ref):
    lc, r, nb, lanes = S5_CHUNK, S5_BLOCK, N_BATCH, V7X_LANES
    nblk = lc // r
    brows = nblk * nb
    prow = S5_PERM_STEPS * nb

    @pl.when(pl.program_id(0) == 0)
    def _():
        st_ref[...] = jnp.zeros(st_ref.shape, F32)

    for d, (u_ref, y_ref) in enumerate(((uf_ref, yf_ref), (ub_ref, yb_ref))):
        for h in range(lc // S5_PERM_STEPS):
            blk = u_ref[:, h * S5_PERM_STEPS:(h + 1) * S5_PERM_STEPS, :].reshape(prow, S5_WIDTH)
            utb_ref[h * prow:(h + 1) * prow, :] = _dot(perm_ref[...], blk.astype(BF16))
        for j in range(nblk):
            for s in range(r):
                slab = utb_ref[(j * r + s) * nb:(j * r + s + 1) * nb, :]
                for kb in range(S5_CH_BLOCKS):
                    up_ref[j * nb:(j + 1) * nb, (kb * r + s) * lanes:(kb * r + s + 1) * lanes] = (
                        slab[:, kb * lanes:(kb + 1) * lanes])

        off = nb if d == 0 else 0
        carry = 0 if d == 0 else brows
        ch_per_btile = S5_BTILE // S5_STATE * S5_GROUP
        for j in range(S5_NSTATE // S5_BTILE):
            kb = (j * ch_per_btile) // lanes
            cols = slice(j * S5_BTILE, (j + 1) * S5_BTILE)
            lhs = up_ref[:, kb * r * lanes:(kb + 1) * r * lanes].astype(BF16)
            sre_ref[off:off + brows, cols] = _dot(lhs, wb_ref[d, 0, j])
            sim_ref[off:off + brows, cols] = _dot(lhs, wb_ref[d, 1, j])

        for cb in range(S5_NSTATE // S5_COLS):
            cols = slice(cb * S5_COLS, (cb + 1) * S5_COLS)
            a_re = jnp.broadcast_to(are_ref[d:d + 1, cols], (nb, S5_COLS))
            a_im = jnp.broadcast_to(aim_ref[d:d + 1, cols], (nb, S5_COLS))
            h_re = st_ref[d, 0, :, cols]
            h_im = st_ref[d, 1, :, cols]
            sre_ref[carry:carry + nb, cols] = h_re
            sim_ref[carry:carry + nb, cols] = h_im
            for jj in range(nblk):
                j = jj if d == 0 else nblk - 1 - jj
                rows = slice(off + j * nb, off + (j + 1) * nb)
                h_re, h_im = (a_re * h_re - a_im * h_im + sre_ref[rows, cols],
                              a_re * h_im + a_im * h_re + sim_ref[rows, cols])
                sre_ref[rows, cols] = h_re
                sim_ref[rows, cols] = h_im
            st_ref[d, 0, :, cols] = h_re
            st_ref[d, 1, :, cols] = h_im

        enter = slice(0, brows) if d == 0 else slice(nb, brows + nb)
        for m in range(S5_CH_BLOCKS):
            srows = slice(m * S5_CTILE, (m + 1) * S5_CTILE)
            ym = (_dot(sre_ref[enter, srows].astype(BF16), wq_ref[d, 0, m])
                  + _dot(sim_ref[enter, srows].astype(BF16), wq_ref[d, 1, m])
                  + _dot(up_ref[:, m * r * lanes:(m + 1) * r * lanes].astype(BF16), wt_ref[d, m]))
            for j in range(nblk):
                for s in range(r):
                    ytb_ref[(j * r + s) * nb:(j * r + s + 1) * nb, m * lanes:(m + 1) * lanes] = (
                        ym[j * nb:(j + 1) * nb, s * lanes:(s + 1) * lanes])

        for h in range(lc // S5_PERM_STEPS):
            y = ytb_ref[h * prow:(h + 1) * prow, :]
            y_hi = y.astype(BF16)
            y_lo = (y - y_hi.astype(F32)).astype(BF16)
            y = _dot(permt_ref[...], y_hi) + _dot(permt_ref[...], y_lo)
            y_ref[:, h * S5_PERM_STEPS:(h + 1) * S5_PERM_STEPS, :] = y.reshape(nb, S5_PERM_STEPS, S5_WIDTH)


def _time_major_permutation():
    n = S5_PERM_STEPS * N_BATCH
    r = jnp.arange(n)
    src = (r % N_BATCH) * S5_PERM_STEPS + r // N_BATCH
    return jax.nn.one_hot(src, n, dtype=BF16)


def _s5_scan(ua, seq_len, wb, wq, wt, a_re, a_im):
    nc = seq_len // S5_CHUNK
    rows = S5_CHUNK * N_BATCH
    brows = rows // S5_BLOCK
    ua = ua.reshape(N_BATCH, seq_len, S5_WIDTH)
    perm = _time_major_permutation()
    fwd = pl.BlockSpec((N_BATCH, S5_CHUNK, S5_WIDTH), lambda c: (0, c, 0))
    bwd = pl.BlockSpec((N_BATCH, S5_CHUNK, S5_WIDTH), lambda c: (0, nc - 1 - c, 0))
    yf, yb = pl.pallas_call(
        _s5_scan_kernel,
        grid=(nc,),
        in_specs=[fwd, bwd, _resident(perm.shape), _resident(perm.shape), _resident(wb.shape),
                  _resident(wq.shape), _resident(wt.shape), _resident(a_re.shape), _resident(a_im.shape)],
        out_specs=[fwd, bwd],
        out_shape=[jax.ShapeDtypeStruct((N_BATCH, seq_len, S5_WIDTH), F32)] * 2,
        scratch_shapes=[pltpu.VMEM((rows, S5_WIDTH), F32),
                        pltpu.VMEM((brows, S5_BLOCK * S5_WIDTH), F32),
                        pltpu.VMEM((brows + N_BATCH, S5_NSTATE), F32),
                        pltpu.VMEM((brows + N_BATCH, S5_NSTATE), F32),
                        pltpu.VMEM((rows, S5_WIDTH), F32),
                        pltpu.VMEM((2, 2, N_BATCH, S5_NSTATE), F32)],
        compiler_params=_cparams(("arbitrary",)),
        name="s5_scan",
    )(ua, ua, perm, perm.T, wb, wq, wt, a_re, a_im)
    n = N_BATCH * seq_len
    return yf.reshape(n, S5_WIDTH), yb.reshape(n, S5_WIDTH)


def _mix_out(x, yf_ref, ybk_ref, ua_ref, sg_ref, d_ref, wglu_ref, woa_ref, wob_ref):
    y = yf_ref[...] + ybk_ref[...] + d_ref[...] * ua_ref[...]
    g = _gelu(y)
    ya = g * _sigmoid(_dot(g.astype(BF16), wglu_ref[...]))
    return x + _dot(ya.astype(BF16), woa_ref[...]) + _dot(sg_ref[...], wob_ref[...])


def _pool(x, xp_ref, xn_ref, g_ref, pw_ref, ps_ref, hp_ref, sa_ref, sb_ref, *, seq_len):
    tm = x.shape[0]
    halo = POOL_HALO
    tiles_per_seq = seq_len // tm
    it = pl.program_id(0) % tiles_per_seq
    g = g_ref[...]
    h = _rms(x, g)
    keep_prev = (it > 0).astype(F32)
    keep_next = (it < tiles_per_seq - 1).astype(F32)
    hp_ref[0:halo, :] = _rms(xp_ref[...], g) * keep_prev
    hp_ref[halo:halo + tm, :] = h
    hp_ref[halo + tm:, :] = _rms(xn_ref[...], g) * keep_next
    lo, n = halo // 2, tm + halo
    tail = slice(lo + n, tm + 2 * halo)
    sa_ref[tail, :] = jnp.zeros((halo // 2, D_MODEL), F32)
    sb_ref[tail, :] = jnp.zeros((halo // 2, D_MODEL), F32)

    def doubled(src_ref, dst_ref, step, cols):
        dst_ref[lo:lo + n, cols] = src_ref[lo:lo + n, cols] + src_ref[lo + step:lo + step + n, cols]

    def window(ref, first, cols):
        return ref[first:first + tm, cols] + ref[halo:halo + tm, cols]

    t = it * tm + lax.broadcasted_iota(jnp.int32, (tm, 1), 0)
    outs = []
    for gi, w in enumerate(POOL_WINDOWS):
        cols = slice(gi * POOL_GROUP_DIM, (gi + 1) * POOL_GROUP_DIM)
        if w == 2:
            s = window(hp_ref, halo - 1, cols)
        elif w == 4:
            doubled(hp_ref, sa_ref, 1, cols)
            s = window(sa_ref, halo - 2, cols)
        elif w == 8:
            doubled(hp_ref, sa_ref, 1, cols)
            doubled(sa_ref, sb_ref, 2, cols)
            s = window(sb_ref, halo - 4, cols)
        else:
            assert w == 16
            doubled(hp_ref, sa_ref, 1, cols)
            doubled(sa_ref, sb_ref, 2, cols)
            doubled(sb_ref, sa_ref, 4, cols)
            s = window(sa_ref, halo - 8, cols)
        lo_t = jnp.clip(t - w // 2, 0, seq_len)
        hi_t = jnp.clip(t + w // 2, 0, seq_len)
        inv_cnt = 1.0 / (hi_t - lo_t).astype(F32)
        pooled = s * inv_cnt - h[:, cols]
        outs.append(_dot(pooled.astype(BF16), pw_ref[gi]))
    return x + jnp.concatenate(outs, axis=-1) * ps_ref[...]


def _kv_kernel(m_ref, g_ref, wkv_ref, kt_ref, v_ref):
    hm = _rms(m_ref[...], g_ref[...]).astype(BF16)
    kv = _dot(hm, wkv_ref[...])
    kt_ref[...] = kv[:, :D_MODEL].T.astype(BF16)
    v_ref[...] = kv[:, D_MODEL:].astype(BF16)


def _kv(mem, layer, g, wkv):
    return pl.pallas_call(
        _kv_kernel,
        grid=(N_BATCH,),
        in_specs=[pl.BlockSpec((None, N_MEM, D_MODEL), lambda b: (b, 0, 0)),
                  _resident((1, D_MODEL)), _resident_layer(wkv.shape, layer)],
        out_specs=[pl.BlockSpec((None, D_MODEL, N_MEM), lambda b: (b, 0, 0)),
                   pl.BlockSpec((None, N_MEM, D_MODEL), lambda b: (b, 0, 0))],
        out_shape=[jax.ShapeDtypeStruct((N_BATCH, D_MODEL, N_MEM), BF16),
                   jax.ShapeDtypeStruct((N_BATCH, N_MEM, D_MODEL), BF16)],
        compiler_params=_cparams(("arbitrary",)),
        name="mem_kv",
    )(mem, g, wkv)


def _cross_attn(x, g_ref, wq_ref, kt_ref, v_ref, wo_ref):
    h = _rms(x, g_ref[...]).astype(BF16)
    q = (_dot(h, wq_ref[...]) * (CROSS_HEAD_DIM ** -0.5)).astype(BF16)
    heads = []
    for hd in range(CROSS_HEADS):
        cols = slice(hd * CROSS_HEAD_DIM, (hd + 1) * CROSS_HEAD_DIM)
        s = _dot(q[:, cols], kt_ref[cols, :])
        e = jnp.exp(s - jnp.max(s, axis=-1, keepdims=True))
        p = e * (1.0 / jnp.sum(e, axis=-1, keepdims=True))
        heads.append(_dot(p.astype(BF16), v_ref[:, cols]).astype(BF16))
    o = jnp.concatenate(heads, axis=-1)
    return x + _dot(o, wo_ref[...])


def _mix_cross_kernel(x_ref, yf_ref, ybk_ref, ua_ref, sg_ref, d_ref, wglu_ref, woa_ref, wob_ref,
                      g_ref, wq_ref, kt_ref, v_ref, wo_ref, o_ref):
    x = _mix_out(x_ref[...], yf_ref, ybk_ref, ua_ref, sg_ref, d_ref, wglu_ref, woa_ref, wob_ref)
    o_ref[...] = _cross_attn(x, g_ref, wq_ref, kt_ref, v_ref, wo_ref)


def _pool_cross_kernel(x_ref, xp_ref, xn_ref, mg_ref, pw_ref, ps_ref,
                       g_ref, wq_ref, kt_ref, v_ref, wo_ref, o_ref, hp_ref, sa_ref, sb_ref, *, seq_len):
    x = _pool(x_ref[...], xp_ref, xn_ref, mg_ref, pw_ref, ps_ref, hp_ref, sa_ref, sb_ref, seq_len=seq_len)
    o_ref[...] = _cross_attn(x, g_ref, wq_ref, kt_ref, v_ref, wo_ref)


def _cross_specs(seq_len, tm, layer, wq, wo):
    tiles_per_seq = seq_len // tm
    return [_resident((1, D_MODEL)), _resident_layer(wq.shape, layer),
            pl.BlockSpec((None, D_MODEL, N_MEM), lambda i: (i // tiles_per_seq, 0, 0)),
            pl.BlockSpec((None, N_MEM, D_MODEL), lambda i: (i // tiles_per_seq, 0, 0)),
            _resident_layer(wo.shape, layer)]


def _mix_cross(x, seq_len, yf, ybk, ua, sg, d, wglu, woa, wob, layer, g, wq, kt, v, wo):
    tm = TOK_ROWS
    n = x.shape[0]
    return pl.pallas_call(
        _mix_cross_kernel,
        grid=(n // tm,),
        in_specs=[_row_spec(tm, D_MODEL), _row_spec(tm, S5_WIDTH), _row_spec(tm, S5_WIDTH),
                  _row_spec(tm, S5_WIDTH), _row_spec(tm, SGU_WIDTH),
                  _resident((1, S5_WIDTH)), _resident(wglu.shape), _resident(woa.shape),
                  _resident(wob.shape)] + _cross_specs(seq_len, tm, layer, wq, wo),
        out_specs=_row_spec(tm, D_MODEL),
        out_shape=jax.ShapeDtypeStruct((n, D_MODEL), F32),
        compiler_params=_cparams(("arbitrary",)),
        name="mix_cross",
    )(x, yf, ybk, ua, sg, d, wglu, woa, wob, g, wq, kt, v, wo)


def _pool_cross(x, seq_len, mg, pw, ps, layer, g, wq, kt, v, wo):
    tm = TOK_ROWS
    n = x.shape[0]
    per = tm // POOL_HALO
    last = n // POOL_HALO - 1
    prev = pl.BlockSpec((POOL_HALO, D_MODEL), lambda i: (jnp.maximum(i * per - 1, 0), 0))
    nxt = pl.BlockSpec((POOL_HALO, D_MODEL), lambda i: (jnp.minimum((i + 1) * per, last), 0))
    padded = pltpu.VMEM((tm + 2 * POOL_HALO, D_MODEL), F32)
    return pl.pallas_call(
        functools.partial(_pool_cross_kernel, seq_len=seq_len),
        grid=(n // tm,),
        in_specs=[_row_spec(tm, D_MODEL), prev, nxt, _resident((1, D_MODEL)), _resident(pw.shape),
                  _resident((1, D_MODEL))] + _cross_specs(seq_len, tm, layer, wq, wo),
        out_specs=_row_spec(tm, D_MODEL),
        out_shape=jax.ShapeDtypeStruct((n, D_MODEL), F32),
        scratch_shapes=[padded, padded, padded],
        compiler_params=_cparams(("arbitrary",)),
        name="pool_cross",
    )(x, x, x, mg, pw, ps, g, wq, kt, v, wo)


def _trunk(x, mem, p):
    n_batch, seq_len, _ = x.shape
    assert n_batch == N_BATCH and seq_len % max(FFN_ROWS, TOK_ROWS) == 0 and seq_len % S5_CHUNK == 0
    x = x.reshape(n_batch * seq_len, D_MODEL)
    row = lambda a: a.reshape(1, -1)
    fg = row(p['final_norm'])
    for i in range(2):
        x = _ffn(x, i, row(p['ffn1_norm'][i]), p['ffn1_w_gate'], p['ffn1_w_up'],
                 p['ffn1_w_down'], fg, name=f"ffn1_l{i}")
        mg = row(p['mix_norm'][i])
        kt, v = _kv(mem, i, row(p['mem_norm'][i]), p['cross_w_kv'])
        cross = (i, row(p['cross_norm'][i]), p['cross_w_q'], kt, v, p['cross_w_o'])
        if i == 0:
            ua, sg = _inproj_sgu(x, mg, p['ab_w_in'], row(p['sgu_norm_g']), row(p['sgu_norm_b']),
                                 p['sgu_w_s'], p['sgu_b_s'])
            yf, ybk = _s5_scan(ua, seq_len, p['s5_wb'], p['s5_wq'], p['s5_wt'], p['s5_a_re'], p['s5_a_im'])
            x = _mix_cross(x, seq_len, yf, ybk, ua, sg, row(p['s5_d']), p['s5_w_glu'],
                           p['ab_w_out_a'], p['ab_w_out_b'], *cross)
        else:
            x = _pool_cross(x, seq_len, mg, p['pool_w'], row(p['pool_scale']), *cross)
        x = _ffn(x, i, row(p['ffn2_norm'][i]), p['ffn2_w_gate'], p['ffn2_w_up'],
                 p['ffn2_w_down'], fg, final_norm=(i == 1), name=f"ffn2_l{i}")
    return x.reshape(n_batch, seq_len, D_MODEL)


def kernel(x_prompt, x_sample, mem_prompt, mem_sample, ffn1_norm, ffn1_w_gate, ffn1_w_up, ffn1_w_down, mix_norm, ab_w_in, s5_lambda_re, s5_lambda_im, s5_log_dt, s5_b_re, s5_b_im, s5_c_re, s5_c_im, s5_d, s5_w_glu, sgu_norm_g, sgu_norm_b, sgu_w_s, sgu_b_s, ab_w_out, pool_w, pool_scale, cross_norm, mem_norm, cross_w_q, cross_w_kv, cross_w_o, ffn2_norm, ffn2_w_gate, ffn2_w_up, ffn2_w_down, final_norm):
    bf = lambda a: a.astype(BF16)
    ar_re, ar_im, wb, wq, wt = _s5_prepare(
        s5_lambda_re[0], s5_lambda_im[0], s5_log_dt[0], s5_b_re[0], s5_b_im[0], s5_c_re[0], s5_c_im[0])
    p = dict(
        ffn1_norm=ffn1_norm, ffn1_w_gate=bf(ffn1_w_gate), ffn1_w_up=bf(ffn1_w_up),
        ffn1_w_down=bf(ffn1_w_down), mix_norm=mix_norm, ab_w_in=bf(ab_w_in[0]),
        s5_wb=wb, s5_wq=wq, s5_wt=wt, s5_a_re=ar_re, s5_a_im=ar_im,
        s5_d=s5_d[0], s5_w_glu=bf(s5_w_glu[0]),
        sgu_norm_g=sgu_norm_g[0], sgu_norm_b=sgu_norm_b[0], sgu_w_s=bf(sgu_w_s[0]),
        sgu_b_s=jnp.broadcast_to(sgu_b_s[0][:, :, None], (SGU_HEADS, SGU_CHUNK, SGU_HEAD_DIM)),
        ab_w_out_a=bf(ab_w_out[0, :S5_WIDTH]), ab_w_out_b=bf(ab_w_out[0, S5_WIDTH:]),
        pool_w=bf(pool_w[0]), pool_scale=pool_scale[0],
        cross_norm=cross_norm, mem_norm=mem_norm, cross_w_q=bf(cross_w_q), cross_w_kv=bf(cross_w_kv),
        cross_w_o=bf(cross_w_o), ffn2_norm=ffn2_norm, ffn2_w_gate=bf(ffn2_w_gate),
        ffn2_w_up=bf(ffn2_w_up), ffn2_w_down=bf(ffn2_w_down), final_norm=final_norm)
    return (_trunk(x_prompt, mem_prompt, p), _trunk(x_sample, mem_sample, p))
```
